```python
import jax
import jax.numpy as jnp
from jax import lax
import numpy as np

D_MODEL = 1024
BATCH = 4
SEQ = 4096
DEPTH = 2
DEC_BATCH = 128
DEC_SEQ = 8
PAST_LEN = 8192
PAGE_SIZE = 128

MLA_HEADS = 8
MLA_NOPE = 64
MLA_ROPE = 32
MLA_V = 64
MLA_Q_LORA = 256
MLA_KV_LORA = 256
MOBA_HEADS = 4
MOBA_DIM = 64
MOBA_BLOCK = 256
MOBA_TOPK = 3
GM_GROUPS = 4
GM_DIM = 64
GM_CHUNK = 128
PEER_HEADS = 8
PEER_NKEYS = 128
PEER_EXPERTS = PEER_NKEYS * PEER_NKEYS
PEER_QDIM = 256
PEER_HALF = PEER_QDIM // 2
PEER_TOPK = 16
PEER_TOKEN_BLOCK = 256
N_BRANCH = 3
ROPE_THETA = 10000.0
EPS = 1e-6
Q_BLOCK = 128
NEG_INF = -1e30

MOBA_W = MOBA_HEADS * MOBA_DIM
GM_W = GM_GROUPS * GM_DIM
IN_SIZES = (MLA_Q_LORA, MLA_KV_LORA, MLA_ROPE, MOBA_W, MOBA_W, MOBA_W, GM_W, GM_W, N_BRANCH * D_MODEL)
IN_COLS = sum(IN_SIZES)
IN_OFFSETS = tuple(int(o) for o in np.cumsum(IN_SIZES)[:-1])

kernel_name = 'hybrid_mla_moba_gmlp_peer_step'


def rmsnorm(x, g):
    xf = x.astype(jnp.float32)
    y = xf * lax.rsqrt(jnp.mean(xf * xf, axis=-1, keepdims=True) + EPS)
    return (y * g.astype(jnp.float32)).astype(x.dtype)


def layernorm(x, g, b):
    xf = x.astype(jnp.float32)
    xc = xf - jnp.mean(xf, axis=-1, keepdims=True)
    y = xc * lax.rsqrt(jnp.mean(xc * xc, axis=-1, keepdims=True) + EPS)
    return (y * g.astype(jnp.float32) + b.astype(jnp.float32)).astype(x.dtype)


def rope(x, pos):
    d = x.shape[-1]
    inv = ROPE_THETA ** (-jnp.arange(0, d, 2, dtype=jnp.float32) / d)
    ang = pos.astype(jnp.float32)[:, None] * inv[None, :]
    ang = ang.reshape(ang.shape[:1] + (1,) * (x.ndim - 3) + ang.shape[1:])
    cos, sin = jnp.cos(ang), jnp.sin(ang)
    xf = x.astype(jnp.float32)
    x1, x2 = xf[..., : d // 2], xf[..., d // 2:]
    return jnp.concatenate([x1 * cos - x2 * sin, x1 * sin + x2 * cos], axis=-1).astype(x.dtype)


def branch_inputs(h, pos, w_in, g_q, w_uq, g_kv, g_gv, b_gv):
    b, s, _ = h.shape
    z = h @ w_in
    q_lat, kv_lat, k_rope, m_q, m_k, m_v, gm_u, gm_v, gates = jnp.split(z, IN_OFFSETS, axis=-1)
    q = (rmsnorm(q_lat, g_q) @ w_uq).reshape(b, s, MLA_HEADS, MLA_NOPE + MLA_ROPE)
    q_nope = q[..., :MLA_NOPE]
    q_pe = rope(q[..., MLA_NOPE:], pos)
    c_kv = rmsnorm(kv_lat, g_kv)
    k_pe = rope(k_rope, pos)
    mq = rope(m_q.reshape(b, s, MOBA_HEADS, MOBA_DIM), pos)
    mk = rope(m_k.reshape(b, s, MOBA_HEADS, MOBA_DIM), pos)
    mv = m_v.reshape(b, s, MOBA_HEADS, MOBA_DIM)
    u = jax.nn.gelu(gm_u)
    v = layernorm(jax.nn.gelu(gm_v), g_gv, b_gv).reshape(b, s, GM_GROUPS, GM_DIM)
    gates = jax.nn.sigmoid(gates).reshape(b, s, N_BRANCH, D_MODEL)
    return q_nope, q_pe, c_kv, k_pe, mq, mk, mv, u, v, gates


def mla_prompt(q_nope, q_pe, c_kv, k_pe, w_ukv):
    b, s = q_nope.shape[:2]
    kv = (c_kv @ w_ukv).reshape(b, s, MLA_HEADS, MLA_NOPE + MLA_V)
    k_nope, v = kv[..., :MLA_NOPE], kv[..., MLA_NOPE:]
    scale = (MLA_NOPE + MLA_ROPE) ** -0.5
    nq = s // Q_BLOCK
    qn = q_nope.reshape(b, nq, Q_BLOCK, MLA_HEADS, MLA_NOPE).transpose(1, 0, 2, 3, 4)
    qp = q_pe.reshape(b, nq, Q_BLOCK, MLA_HEADS, MLA_ROPE).transpose(1, 0, 2, 3, 4)
    kpos = jnp.arange(s)

    def block(args):
        i, qn_b, qp_b = args
        sc = jnp.einsum('bqhd,bkhd->bhqk', qn_b, k_nope) + jnp.einsum('bqhr,bkr->bhqk', qp_b, k_pe)
        sc = sc.astype(jnp.float32) * scale
        qpos = i * Q_BLOCK + jnp.arange(Q_BLOCK)
        sc = jnp.where(kpos[None, :] <= qpos[:, None], sc, NEG_INF)
        p = jax.nn.softmax(sc, axis=-1).astype(v.dtype)
        return jnp.einsum('bhqk,bkhd->bqhd', p, v)

    o = lax.map(block, (jnp.arange(nq), qn, qp))
    return o.transpose(1, 0, 2, 3, 4).reshape(b, s, MLA_HEADS, MLA_V)


def mla_sample(q_nope, q_pe, c_kv, k_pe, cache_lat, cache_pe, page_table, layer, w_ukv):
    db, t = q_nope.shape[:2]
    past = page_table.shape[1] * PAGE_SIZE
    lat_p = cache_lat[layer, page_table].reshape(db, past, MLA_KV_LORA)
    pe_p = cache_pe[layer, page_table].reshape(db, past, MLA_ROPE)
    w = w_ukv.reshape(MLA_KV_LORA, MLA_HEADS, MLA_NOPE + MLA_V)
    w_uk, w_uv = w[..., :MLA_NOPE], w[..., MLA_NOPE:]
    scale = (MLA_NOPE + MLA_ROPE) ** -0.5
    q_abs = jnp.einsum('bthd,chd->bthc', q_nope, w_uk)
    s_p = (jnp.einsum('bthc,blc->bhtl', q_abs, lat_p) + jnp.einsum('bthr,blr->bhtl', q_pe, pe_p)).astype(jnp.float32) * scale
    s_n = (jnp.einsum('bthc,bsc->bhts', q_abs, c_kv) + jnp.einsum('bthr,bsr->bhts', q_pe, k_pe)).astype(jnp.float32) * scale
    causal = jnp.arange(t)[None, :] <= jnp.arange(t)[:, None]
    s_n = jnp.where(causal, s_n, NEG_INF)
    p = jax.nn.softmax(jnp.concatenate([s_p, s_n], axis=-1), axis=-1).astype(c_kv.dtype)
    o_lat = jnp.einsum('bhtl,blc->bthc', p[..., :past], lat_p) + jnp.einsum('bhts,bsc->bthc', p[..., past:], c_kv)
    return jnp.einsum('bthc,chd->bthd', o_lat, w_uv)


def moba_prompt(q, k, v):
    b, s, h, d = q.shape
    nblk = -(-s // MOBA_BLOCK)
    pad = nblk * MOBA_BLOCK - s
    padw = ((0, 0), (0, pad), (0, 0), (0, 0))
    kb = jnp.pad(k, padw).reshape(b, nblk, MOBA_BLOCK, h, d).transpose(0, 3, 1, 2, 4)
    vb = jnp.pad(v, padw).reshape(b, nblk, MOBA_BLOCK, h, d).transpose(0, 3, 1, 2, 4)
    kmean = jnp.mean(kb.astype(jnp.float32), axis=3)
    ksel = min(MOBA_TOPK, (s - 1) // MOBA_BLOCK)
    scale = d ** -0.5
    nq = s // Q_BLOCK
    qb = q.reshape(b, nq, Q_BLOCK, h, d).transpose(1, 0, 3, 2, 4)
    bi = jnp.arange(b)[:, None, None, None]
    hi = jnp.arange(h)[None, :, None, None]
    blk_ids = jnp.arange(nblk)

    def block(args):
        c, qc = args
        qpos = c * Q_BLOCK + jnp.arange(Q_BLOCK)
        own = (c * Q_BLOCK) // MOBA_BLOCK
        k_own = lax.dynamic_index_in_dim(kb, own, axis=2, keepdims=False)
        v_own = lax.dynamic_index_in_dim(vb, own, axis=2, keepdims=False)
        kpos = own * MOBA_BLOCK + jnp.arange(MOBA_BLOCK)
        s_own = jnp.einsum('bhqd,bhkd->bhqk', qc, k_own).astype(jnp.float32) * scale
        s_own = jnp.where(kpos[None, :] <= qpos[:, None], s_own, NEG_INF)
        if ksel == 0:
            p = jax.nn.softmax(s_own, axis=-1).astype(v.dtype)
            return jnp.einsum('bhqk,bhkd->bhqd', p, v_own)
        gate = jnp.einsum('bhqd,bhnd->bhqn', qc.astype(jnp.float32), kmean)
        gate = jnp.where(blk_ids < own, gate, NEG_INF)
        _, idx = lax.top_k(gate, ksel)
        k_sel = kb[bi, hi, idx]
        v_sel = vb[bi, hi, idx]
        s_sel = jnp.einsum('bhqd,bhqnkd->bhqnk', qc, k_sel).astype(jnp.float32) * scale
        s_sel = jnp.where((jnp.arange(ksel) < own)[:, None], s_sel, NEG_INF)
        nsel = ksel * MOBA_BLOCK
        p = jax.nn.softmax(jnp.concatenate([s_sel.reshape(b, h, Q_BLOCK, nsel), s_own], axis=-1), axis=-1).astype(v.dtype)
        p_sel = p[..., :nsel].reshape(b, h, Q_BLOCK, ksel, MOBA_BLOCK)
        return jnp.einsum('bhqnk,bhqnkd->bhqd', p_sel, v_sel) + jnp.einsum('bhqk,bhkd->bhqd', p[..., nsel:], v_own)

    o = lax.map(block, (jnp.arange(nq), qb))
    return o.transpose(1, 0, 3, 2, 4).reshape(b, s, h, d)


def moba_sample(q, k_new, v_new, cache_k, cache_v, page_table, layer):
    db, t, h, d = q.shape
    past = page_table.shape[1] * PAGE_SIZE
    nb = past // MOBA_BLOCK
    own_start = nb * MOBA_BLOCK
    r = past - own_start
    ppb = MOBA_BLOCK // PAGE_SIZE
    scale = d ** -0.5
    qh = q.transpose(0, 2, 1, 3)
    qpos = past + jnp.arange(t)
    own_pages = page_table[:, own_start // PAGE_SIZE:]
    k_own = jnp.concatenate([cache_k[layer, own_pages].reshape(db, r, h, d), k_new], axis=1)
    v_own = jnp.concatenate([cache_v[layer, own_pages].reshape(db, r, h, d), v_new], axis=1)
    kpos = own_start + jnp.arange(r + t)
    s_own = jnp.einsum('bhtd,bkhd->bhtk', qh, k_own).astype(jnp.float32) * scale
    s_own = jnp.where(kpos[None, :] <= qpos[:, None], s_own, NEG_INF)
    if nb == 0:
        p = jax.nn.softmax(s_own, axis=-1).astype(v_new.dtype)
        return jnp.einsum('bhtk,bkhd->bhtd', p, v_own).transpose(0, 2, 1, 3)
    ksel = min(MOBA_TOPK, nb)
    k_full = cache_k[layer, page_table[:, :nb * ppb]].reshape(db, nb, MOBA_BLOCK, h, d)
    kmean = jnp.mean(k_full.astype(jnp.float32), axis=2)
    gate = jnp.einsum('bhtd,bnhd->bhtn', qh.astype(jnp.float32), kmean)
    _, idx = lax.top_k(gate, ksel)
    logical = idx[..., None] * ppb + jnp.arange(ppb)
    phys = page_table[jnp.arange(db)[:, None, None, None, None], logical]
    rows = jnp.arange(PAGE_SIZE)
    hsel = jnp.arange(h)[None, :, None, None, None, None]
    nsel = ksel * MOBA_BLOCK
    k_sel = cache_k[layer, phys[..., None], rows, hsel].reshape(db, h, t, nsel, d)
    v_sel = cache_v[layer, phys[..., None], rows, hsel].reshape(db, h, t, nsel, d)
    s_sel = jnp.einsum('bhtd,bhtkd->bhtk', qh, k_sel).astype(jnp.float32) * scale
    p = jax.nn.softmax(jnp.concatenate([s_sel, s_own], axis=-1), axis=-1).astype(v_new.dtype)
    o = jnp.einsum('bhtk,bhtkd->bhtd', p[..., :nsel], v_sel) + jnp.einsum('bhtk,bkhd->bhtd', p[..., nsel:], v_own)
    return o.transpose(0, 2, 1, 3)


def causal_ws(w_s):
    return jnp.where(jnp.tril(jnp.ones((GM_CHUNK, GM_CHUNK), dtype=bool)), w_s, 0)


def gmlp_prompt(u, v, w_s, b_s):
    b, s = u.shape[:2]
    nc = s // GM_CHUNK
    vc = v.reshape(b, nc, GM_CHUNK, GM_GROUPS, GM_DIM)
    mixed = jnp.einsum('gts,bcsgd->bctgd', causal_ws(w_s), vc) + b_s.T[:, :, None]
    return u * mixed.reshape(b, s, GM_W)


def gmlp_sample(u, v, w_s, b_s):
    db, t = u.shape[:2]
    ws = causal_ws(w_s)[:, :t, :t]
    mixed = jnp.einsum('gts,bsgd->btgd', ws, v) + b_s[:, :t].T[:, :, None]
    return u * mixed.reshape(db, t, GM_W)


def merge_branches(a, b, c, gates, w_pa, w_pb, w_pc, w_out):
    bs, s = a.shape[:2]
    ya = a.reshape(bs, s, MLA_HEADS * MLA_V) @ w_pa
    yb = b.reshape(bs, s, MOBA_W) @ w_pb
    yc = c @ w_pc
    m = gates[:, :, 0] * ya + gates[:, :, 1] * yb + gates[:, :, 2] * yc
    return m @ w_out


def peer_ffn(hn, w_q, sub_keys, u_tab, v_tab):
    shp = hn.shape
    xf = hn.reshape(-1, D_MODEL)
    n = xf.shape[0]
    pad = (-n) % PEER_TOKEN_BLOCK
    xb = jnp.pad(xf, ((0, pad), (0, 0))).reshape(-1, PEER_TOKEN_BLOCK, D_MODEL)
    tb, kk = PEER_TOKEN_BLOCK, PEER_TOPK

    def block(xt):
        q = (xt @ w_q).reshape(tb, PEER_HEADS, 2, PEER_HALF)
        sc = jnp.einsum('thpc,hpkc->thpk', q, sub_keys).astype(jnp.float32)
        sv, si = lax.top_k(sc, kk)
        cand_s = (sv[:, :, 0, :, None] + sv[:, :, 1, None, :]).reshape(tb, PEER_HEADS, kk * kk)
        cand_i = (si[:, :, 0, :, None] * PEER_NKEYS + si[:, :, 1, None, :]).reshape(tb, PEER_HEADS, kk * kk)
        top_s, top_j = lax.top_k(cand_s, kk)
        eid = jnp.take_along_axis(cand_i, top_j, axis=-1)
        g = jax.nn.softmax(top_s, axis=-1)
        act = jax.nn.gelu(jnp.einsum('thkd,td->thk', u_tab[eid], xt).astype(jnp.float32))
        return jnp.einsum('thk,thkd->td', (g * act).astype(xt.dtype), v_tab[eid])

    y = lax.map(block, xb).reshape(-1, D_MODEL)[:n]
    return y.reshape(shp)


def setup_inputs(seed: int = 0) -> dict:
    key = jax.random.key(seed)
    keys = list(jax.random.split(key, 32))
    f32 = jnp.float32

    def normal(shape, scale):
        return jax.random.normal(keys.pop(), shape, f32) * scale

    def gain(shape):
        return 1.0 + 0.05 * jax.random.normal(keys.pop(), shape, f32)

    n_pages = PAST_LEN // PAGE_SIZE
    n_used = DEC_BATCH * n_pages
    n_pool = n_used + (n_used + 3) // 4
    page_table = jax.random.permutation(keys.pop(), n_pool)[:n_used].reshape(DEC_BATCH, n_pages).astype(jnp.int32)
    L = DEPTH
    return {
        'x_prompt': normal((BATCH, SEQ, D_MODEL), 1.0),
        'x_sample': normal((DEC_BATCH, DEC_SEQ, D_MODEL), 1.0),
        'cache_mla_latent': normal((L, n_pool, PAGE_SIZE, MLA_KV_LORA), 1.0),
        'cache_mla_krope': normal((L, n_pool, PAGE_SIZE, MLA_ROPE), 1.0),
        'cache_moba_k': normal((L, n_pool, PAGE_SIZE, MOBA_HEADS, MOBA_DIM), 1.0),
        'cache_moba_v': normal((L, n_pool, PAGE_SIZE, MOBA_HEADS, MOBA_DIM), 1.0),
        'page_table': page_table,
        'g_mix': gain((L, D_MODEL)),
        'w_in': normal((L, D_MODEL, IN_COLS), D_MODEL ** -0.5),
        'g_q_lat': gain((L, MLA_Q_LORA)),
        'w_uq': normal((L, MLA_Q_LORA, MLA_HEADS * (MLA_NOPE + MLA_ROPE)), MLA_Q_LORA ** -0.5),
        'g_kv_lat': gain((L, MLA_KV_LORA)),
        'w_ukv': normal((L, MLA_KV_LORA, MLA_HEADS * (MLA_NOPE + MLA_V)), MLA_KV_LORA ** -0.5),
        'g_gm_v': gain((L, GM_W)),
        'b_gm_v': normal((L, GM_W), 0.02),
        'w_s': normal((L, GM_GROUPS, GM_CHUNK, GM_CHUNK), GM_CHUNK ** -0.5),
        'b_s': gain((L, GM_GROUPS, GM_CHUNK)),
        'w_pa': normal((L, MLA_HEADS * MLA_V, D_MODEL), (MLA_HEADS * MLA_V) ** -0.5),
        'w_pb': normal((L, MOBA_W, D_MODEL), MOBA_W ** -0.5),
        'w_pc': normal((L, GM_W, D_MODEL), GM_W ** -0.5),
        'w_out': normal((L, D_MODEL, D_MODEL), D_MODEL ** -0.5),
        'g_ffn': gain((L, D_MODEL)),
        'w_peer_q': normal((L, D_MODEL, PEER_HEADS * PEER_QDIM), D_MODEL ** -0.5),
        'peer_keys': normal((L, PEER_HEADS, 2, PEER_NKEYS, PEER_HALF), PEER_HALF ** -0.5),
        'peer_u': normal((L, PEER_EXPERTS, D_MODEL), D_MODEL ** -0.5),
        'peer_v': normal((L, PEER_EXPERTS, D_MODEL), PEER_HEADS ** -0.5),
        'g_final': gain((D_MODEL,)),
    }


def reference(x_prompt, x_sample, cache_mla_latent, cache_mla_krope, cache_moba_k, cache_moba_v, page_table,
              g_mix, w_in, g_q_lat, w_uq, g_kv_lat, w_ukv, g_gm_v, b_gm_v, w_s, b_s,
              w_pa, w_pb, w_pc, w_out, g_ffn, w_peer_q, peer_keys, peer_u, peer_v, g_final):
    s = x_prompt.shape[1]
    t = x_sample.shape[1]
    past = page_table.shape[1] * PAGE_SIZE
    pos_p = jnp.arange(s, dtype=jnp.int32)
    pos_s = past + jnp.arange(t, dtype=jnp.int32)
    xp, xs = x_prompt, x_sample
    lat_p, pe_p, mk_p, mv_p = [], [], [], []
    lat_s, pe_s, mk_s, mv_s, gv_s = [], [], [], [], []
    for l in range(DEPTH):
        hp = rmsnorm(xp, g_mix[l])
        qn, qpe, ckv, kpe, mq, mk, mv, u, v, gates = branch_inputs(
            hp, pos_p, w_in[l], g_q_lat[l], w_uq[l], g_kv_lat[l], g_gm_v[l], b_gm_v[l])
        a = mla_prompt(qn, qpe, ckv, kpe, w_ukv[l])
        bb = moba_prompt(mq, mk, mv)
        c = gmlp_prompt(u, v, w_s[l], b_s[l])
        xp = xp + merge_branches(a, bb, c, gates, w_pa[l], w_pb[l], w_pc[l], w_out[l])
        xp = xp + peer_ffn(rmsnorm(xp, g_ffn[l]), w_peer_q[l], peer_keys[l], peer_u[l], peer_v[l])
        lat_p.append(ckv)
        pe_p.append(kpe)
        mk_p.append(mk)
        mv_p.append(mv)
        hs = rmsnorm(xs, g_mix[l])
        qn, qpe, ckv, kpe, mq, mk, mv, u, v, gates = branch_inputs(
            hs, pos_s, w_in[l], g_q_lat[l], w_uq[l], g_kv_lat[l], g_gm_v[l], b_gm_v[l])
        a = mla_sample(qn, qpe, ckv, kpe, cache_mla_latent, cache_mla_krope, page_table, l, w_ukv[l])
        bb = moba_sample(mq, mk, mv, cache_moba_k, cache_moba_v, page_table, l)
        c = gmlp_sample(u, v, w_s[l], b_s[l])
        xs = xs + merge_branches(a, bb, c, gates, w_pa[l], w_pb[l], w_pc[l], w_out[l])
        xs = xs + peer_ffn(rmsnorm(xs, g_ffn[l]), w_peer_q[l], peer_keys[l], peer_u[l], peer_v[l])
        lat_s.append(ckv)
        pe_s.append(kpe)
        mk_s.append(mk)
        mv_s.append(mv)
        gv_s.append(v)
    y_prompt = rmsnorm(xp, g_final)
    y_sample = rmsnorm(xs, g_final)
    return (y_prompt, y_sample,
            jnp.stack(lat_p), jnp.stack(pe_p), jnp.stack(mk_p), jnp.stack(mv_p),
            jnp.stack(lat_s), jnp.stack(pe_s), jnp.stack(mk_s), jnp.stack(mv_s), jnp.stack(gv_s))
```

```python
import functools

import jax
import jax.numpy as jnp
import numpy as np
from jax import lax
from jax.experimental import pallas as pl
from jax.experimental.pallas import tpu as pltpu

F32 = jnp.float32
BF16 = jnp.bfloat16

MLA_HEADS = 8
MLA_NOPE = 64
MLA_ROPE = 32
MLA_V = 64
MOBA_HEADS = 4
MOBA_DIM = 64
MOBA_BLOCK = 256
MOBA_TOPK = 3
GM_GROUPS = 4
GM_DIM = 64
GM_CHUNK = 128
PEER_HEADS = 8
PEER_TOPK = 16
PAGE_SIZE = 128
ROPE_THETA = 10000.0
EPS = 1e-6
NEG_INF = -1e30

LANES = 128
VMEM_LIMIT = 56 * 1024 * 1024

TOK_TILE = 256
ATT_TILE = 256
SEL_TILE = 256
PEER_TOK_TILE = 512
PEER_ROWS = 4

_NT = (((1,), (1,)), ((), ()))


def _cparams(sem):
    return pltpu.CompilerParams(dimension_semantics=sem, vmem_limit_bytes=VMEM_LIMIT)


def _dot(a, b):
    return jnp.dot(a, b, preferred_element_type=F32)


def _dot_nt(a, b):
    return lax.dot_general(a, b, _NT, preferred_element_type=F32)


def _split(a):
    hi = a.astype(BF16)
    lo = (a - hi.astype(F32)).astype(BF16)
    return hi, lo


def _dot3_nt(a, b):
    ah, al = _split(a)
    bh, bl = _split(b)
    return _dot_nt(ah, bh) + _dot_nt(ah, bl) + _dot_nt(al, bh)


def _rms(x, g):
    return x * lax.rsqrt(jnp.mean(x * x, axis=-1, keepdims=True) + EPS) * g


def _lane_group(shape, width):
    return lax.broadcasted_iota(jnp.int32, shape, len(shape) - 1) // width


_C_QLAT, _C_KVLAT, _C_KPE, _C_KPER = 0, 256, 512, 640
_C_MQ, _C_MQR, _C_MK, _C_MKR, _C_MV, _C_U, _C_V, _C_G = 768, 1024, 1280, 1536, 1792, 2048, 2304, 2560


def _proj_kernel(x_ref, gmix_ref, w_ref, gq_ref, wuq_ref, wuqr_ref, gkv_ref, wukv_ref,
                 ggv_ref, bgv_ref, wmix_ref, bmix_ref,
                 cosq_ref, sinq_ref, cosk_ref, sink_ref, cosm_ref, sinm_ref,
                 ckv_ref, kpe_ref, mk_ref, mv_ref, gv_ref,
                 qall_ref, kvx_ref, kpep_ref, mq_ref, c_ref, gates_ref):
    d_model = x_ref.shape[1]
    hb = _rms(x_ref[...], gmix_ref[...]).astype(BF16)

    def seg(off, width):
        return _dot(hb, w_ref[:, off:off + width])

    qn = _rms(seg(_C_QLAT, 256), gq_ref[...]).astype(BF16)
    cq, sq = cosq_ref[...], sinq_ref[...]
    for h in range(MLA_HEADS):
        sl = slice(h * 256, (h + 1) * 256)
        qa = _dot(qn, wuq_ref[:, sl])
        qr = _dot(qn, wuqr_ref[:, sl])
        qall_ref[:, sl] = (qa * cq + qr * sq).astype(BF16)

    ckv = _rms(seg(_C_KVLAT, 256), gkv_ref[...])
    ckv_ref[...] = ckv
    kvx_ref[...] = _dot(ckv.astype(BF16), wukv_ref[...]).astype(BF16)

    kpe = seg(_C_KPE, LANES) * cosk_ref[...] + seg(_C_KPER, LANES) * sink_ref[...]
    kpep_ref[...] = kpe.astype(BF16)
    kpe_ref[...] = kpe[:, :MLA_ROPE]

    cm, sm = cosm_ref[...], sinm_ref[...]
    mq_ref[...] = seg(_C_MQ, 256) * cm[:, :256] + seg(_C_MQR, 256) * sm[:, :256]
    mk_ref[...] = seg(_C_MK, 256) * cm[:, 256:] + seg(_C_MKR, 256) * sm[:, 256:]
    mv_ref[...] = seg(_C_MV, 256)

    u = jax.nn.gelu(seg(_C_U, 256))
    gvx = jax.nn.gelu(seg(_C_V, 256))
    mu = jnp.mean(gvx, axis=-1, keepdims=True)
    xc = gvx - mu
    v = xc * lax.rsqrt(jnp.mean(xc * xc, axis=-1, keepdims=True) + EPS) * ggv_ref[...] + bgv_ref[...]
    gv_ref[...] = v
    tm = x_ref.shape[0]
    row = lax.broadcasted_iota(jnp.int32, (GM_CHUNK, GM_CHUNK), 0)
    col = lax.broadcasted_iota(jnp.int32, (GM_CHUNK, GM_CHUNK), 1)
    grp = _lane_group((GM_CHUNK, GM_GROUPS * GM_DIM), GM_DIM)
    for ci in range(tm // GM_CHUNK):
        rs = slice(ci * GM_CHUNK, (ci + 1) * GM_CHUNK)
        vc = v[rs]
        mixed = bmix_ref[...]
        for g in range(GM_GROUPS):
            wg = jnp.where(col <= row, wmix_ref[g], 0.0).astype(BF16)
            vg = jnp.where(grp == g, vc, 0.0).astype(BF16)
            mixed = mixed + _dot(wg, vg)
        c_ref[rs, :] = (u[rs] * mixed).astype(BF16)

    for j in range(3):
        gates_ref[:, j * d_model:(j + 1) * d_model] = jax.nn.sigmoid(
            seg(_C_G + j * d_model, d_model)).astype(BF16)


def _rot_cols(w, d):
    k, n = w.shape
    w3 = w.reshape(k, n // d, d)
    return jnp.concatenate([-w3[..., d // 2:], w3[..., :d // 2]], axis=-1).reshape(k, n)


def _rope_tables(pos):
    pos = pos.astype(F32)[:, None]

    def cs(d):
        inv = ROPE_THETA ** (-jnp.arange(0, d, 2, dtype=F32) / d)
        ang = pos * inv[None, :]
        return (jnp.concatenate([jnp.cos(ang)] * 2, axis=1), jnp.concatenate([jnp.sin(ang)] * 2, axis=1))

    n = pos.shape[0]
    c32, s32 = cs(MLA_ROPE)
    c64, s64 = cs(MOBA_DIM)
    ones, zeros = jnp.ones((n, LANES), F32), jnp.zeros((n, LANES), F32)
    pad = jnp.zeros((n, LANES - MLA_ROPE), F32)
    mla_scale = (MLA_NOPE + MLA_ROPE) ** -0.5
    cosq = jnp.concatenate([ones, c32, pad], axis=1) * mla_scale
    sinq = jnp.concatenate([zeros, s32, pad], axis=1) * mla_scale
    cosk = jnp.concatenate([c32, pad], axis=1)
    sink = jnp.concatenate([s32, pad], axis=1)
    moba_scale = MOBA_DIM ** -0.5
    c64h, s64h = jnp.tile(c64, (1, MOBA_HEADS)), jnp.tile(s64, (1, MOBA_HEADS))
    cosm = jnp.concatenate([c64h * moba_scale, c64h], axis=1)
    sinm = jnp.concatenate([s64h * moba_scale, s64h], axis=1)
    return cosq, sinq, cosk, sink, cosm, sinm


def _arrange_w_in(w_in):
    d = w_in.shape[0]
    o = np.cumsum([0, 256, 256, MLA_ROPE, 256, 256, 256, 256, 256])
    q_lat, kv_lat, k_rope, m_q, m_k, m_v, g_u, g_v = (w_in[:, o[i]:o[i + 1]] for i in range(8))
    gates = w_in[:, o[8]:]
    padk = jnp.zeros((d, LANES - MLA_ROPE), w_in.dtype)
    cols = [q_lat, kv_lat, k_rope, padk, _rot_cols(k_rope, MLA_ROPE), padk,
            m_q, _rot_cols(m_q, MOBA_DIM), m_k, _rot_cols(m_k, MOBA_DIM), m_v, g_u, g_v, gates]
    return jnp.concatenate(cols, axis=1).astype(BF16)


def _arrange_w_uq(w_uq):
    k = w_uq.shape[0]
    w3 = w_uq.reshape(k, MLA_HEADS, MLA_NOPE + MLA_ROPE)
    nope, pe = w3[..., :MLA_NOPE], w3[..., MLA_NOPE:]
    z64 = jnp.zeros((k, MLA_HEADS, LANES - MLA_NOPE), w_uq.dtype)
    z96 = jnp.zeros((k, MLA_HEADS, LANES - MLA_ROPE), w_uq.dtype)
    z128 = jnp.zeros((k, MLA_HEADS, LANES), w_uq.dtype)
    pe_rot = jnp.concatenate([-pe[..., MLA_ROPE // 2:], pe[..., :MLA_ROPE // 2]], axis=-1)
    big = jnp.concatenate([nope, z64, pe, z96], axis=-1).reshape(k, MLA_HEADS * 256)
    big_rot = jnp.concatenate([z128, pe_rot, z96], axis=-1).reshape(k, MLA_HEADS * 256)
    return big.astype(BF16), big_rot.astype(BF16)


def _proj_call(x_all, tabs, n_prompt_tiles, g_mix, w_arr, g_q, wuq, wuqr, g_kv, w_ukv, g_gv, b_gv, wmix, bmix):
    t, d = x_all.shape
    tm = TOK_TILE
    nt = t // tm
    row = lambda w: pl.BlockSpec((tm, w), lambda i: (i, 0))
    full = lambda a: pl.BlockSpec(a.shape, lambda i: (0,) * a.ndim)
    kind = lambda i: jnp.where(i >= n_prompt_tiles, 1, 0)
    in_specs = [row(d), full(g_mix), full(w_arr), full(g_q), full(wuq), full(wuqr), full(g_kv), full(w_ukv),
                full(g_gv), full(b_gv),
                pl.BlockSpec((None, GM_GROUPS, GM_CHUNK, GM_CHUNK), lambda i: (kind(i), 0, 0, 0)),
                pl.BlockSpec((None, GM_CHUNK, GM_GROUPS * GM_DIM), lambda i: (kind(i), 0, 0)),
                row(256), row(256), row(LANES), row(LANES), row(512), row(512)]
    outs = [((t, 256), F32), ((t, MLA_ROPE), F32), ((t, 256), F32), ((t, 256), F32), ((t, 256), F32),
            ((t, MLA_HEADS * 256), BF16), ((t, MLA_HEADS * 128), BF16), ((t, LANES), BF16),
            ((t, 256), F32), ((t, 256), BF16), ((t, 3 * d), BF16)]
    return pl.pallas_call(
        _proj_kernel,
        grid=(nt,),
        in_specs=in_specs,
        out_specs=[row(s[1]) for s, _ in outs],
        out_shape=[jax.ShapeDtypeStruct(s, dt) for s, dt in outs],
        compiler_params=_cparams(("parallel",)),
        name="proj",
    )(x_all, g_mix, w_arr, g_q, wuq, wuqr, g_kv, w_ukv, g_gv, b_gv, wmix, bmix, *tabs)


def _mla_prompt_kernel(q_ref, kv_ref, kpe_ref, o_ref):
    tq = q_ref.shape[0]
    qi = pl.program_id(2)
    q = q_ref[...]

    def tile(j, carry, masked):
        m, l, acc = carry
        rows = pl.ds(pl.multiple_of(j * tq, tq), tq)
        kv = kv_ref[rows, :]
        k = jnp.concatenate([kv, kpe_ref[rows, :]], axis=1)
        s = _dot_nt(q, k)
        if masked:
            r = lax.broadcasted_iota(jnp.int32, s.shape, 0)
            c = lax.broadcasted_iota(jnp.int32, s.shape, 1)
            s = jnp.where(c <= r, s, NEG_INF)
        m_new = jnp.maximum(m, jnp.max(s, axis=-1, keepdims=True))
        alpha = jnp.exp(m - m_new)
        p = jnp.exp(s - m_new)
        l = alpha * l + jnp.sum(p, axis=-1, keepdims=True)
        acc = alpha * acc + _dot(p.astype(BF16), kv)
        return m_new, l, acc

    init = (jnp.full((tq, 1), NEG_INF, F32), jnp.zeros((tq, 1), F32), jnp.zeros((tq, LANES), F32))
    carry = tile(qi, init, True)
    m, l, acc = lax.fori_loop(0, qi, lambda j, c: tile(j, c, False), carry)
    o_ref[...] = (acc / l).astype(BF16)


def _mla_prompt_call(qall, kvx, kpep, batch, seq):
    tq = ATT_TILE
    q3 = qall[:batch * seq].reshape(batch, seq, MLA_HEADS * 256)
    kv3 = kvx[:batch * seq].reshape(batch, seq, MLA_HEADS * LANES)
    kp3 = kpep[:batch * seq].reshape(batch, seq, LANES)
    return pl.pallas_call(
        _mla_prompt_kernel,
        grid=(batch, MLA_HEADS, seq // tq),
        in_specs=[pl.BlockSpec((None, tq, 256), lambda b, h, i: (b, i, h)),
                  pl.BlockSpec((None, seq, LANES), lambda b, h, i: (b, 0, h)),
                  pl.BlockSpec((None, seq, LANES), lambda b, h, i: (b, 0, 0))],
        out_specs=pl.BlockSpec((None, tq, LANES), lambda b, h, i: (b, i, h)),
        out_shape=jax.ShapeDtypeStruct((batch, seq, MLA_HEADS * LANES), BF16),
        compiler_params=_cparams(("parallel", "parallel", "arbitrary")),
        name="mla_prompt",
    )(q3, kv3, kp3).reshape(batch * seq, MLA_HEADS * LANES)


def _topk_bias(gate, n_valid, k):
    nblk = gate.shape[1]
    blk = lax.broadcasted_iota(jnp.int32, gate.shape, 1)
    valid = blk < n_valid
    g = jnp.where(valid, gate, NEG_INF)
    rank = jnp.zeros(gate.shape, F32)
    for n in range(nblk):
        gn = g[:, n:n + 1]
        ahead = (gn > g) | ((gn == g) & (n < blk))
        rank = rank + jnp.where(ahead, 1.0, 0.0)
    return jnp.where(valid & (rank < k), 0.0, NEG_INF)


def _stack_heads(q, heads, width):
    grp = _lane_group(q.shape, width)
    return jnp.concatenate([jnp.where(grp == h, q, jnp.zeros_like(q)) for h in range(heads)], axis=0)


def _unstack_heads(o, heads, width):
    rows = o.shape[0] // heads
    grp = _lane_group((rows, o.shape[1]), width)
    out = jnp.zeros((rows, o.shape[1]), F32)
    for h in range(heads):
        out = out + jnp.where(grp == h, o[h * rows:(h + 1) * rows], 0.0)
    return out


def _moba_prompt_kernel(q_ref, k_ref, v_ref, o_ref, kmean_ref, *, ksel):
    tq = q_ref.shape[0]
    nblk = k_ref.shape[0] // MOBA_BLOCK
    qi = pl.program_id(1)

    @pl.when(qi == 0)
    def _():
        for n in range(nblk):
            kmean_ref[n:n + 1, :] = jnp.mean(k_ref[n * MOBA_BLOCK:(n + 1) * MOBA_BLOCK, :], axis=0, keepdims=True)

    qf = _stack_heads(q_ref[...], MOBA_HEADS, MOBA_DIM)
    q = qf.astype(BF16)
    bias = _topk_bias(_dot3_nt(qf, kmean_ref[...]), qi, ksel)
    blk = lax.broadcasted_iota(jnp.int32, bias.shape, 1)

    def tile(j, carry, own):
        m, l, acc = carry
        rows = pl.ds(pl.multiple_of(j * MOBA_BLOCK, MOBA_BLOCK), MOBA_BLOCK)
        s = _dot_nt(q, k_ref[rows, :].astype(BF16))
        if own:
            r = lax.broadcasted_iota(jnp.int32, s.shape, 0) % tq
            c = lax.broadcasted_iota(jnp.int32, s.shape, 1)
            s = jnp.where(c <= r, s, NEG_INF)
        else:
            s = s + jnp.sum(jnp.where(blk == j, bias, 0.0), axis=1, keepdims=True)
        m_new = jnp.maximum(m, jnp.max(s, axis=-1, keepdims=True))
        alpha = jnp.exp(m - m_new)
        p = jnp.exp(s - m_new)
        l = alpha * l + jnp.sum(p, axis=-1, keepdims=True)
        acc = alpha * acc + _dot(p.astype(BF16), v_ref[rows, :].astype(BF16))
        return m_new, l, acc

    n = qf.shape[0]
    init = (jnp.full((n, 1), NEG_INF, F32), jnp.zeros((n, 1), F32), jnp.zeros((n, qf.shape[1]), F32))
    carry = tile(qi, init, True)
    m, l, acc = lax.fori_loop(0, qi, lambda j, c: tile(j, c, False), carry)
    o_ref[...] = _unstack_heads(acc / l, MOBA_HEADS, MOBA_DIM).astype(BF16)


def _moba_prompt_call(mq, mk, mv, batch, seq):
    assert ATT_TILE == MOBA_BLOCK and seq % MOBA_BLOCK == 0
    w = MOBA_HEADS * MOBA_DIM
    r3 = lambda a: a[:batch * seq].reshape(batch, seq, w)
    ksel = min(MOBA_TOPK, (seq - 1) // MOBA_BLOCK)
    return pl.pallas_call(
        functools.partial(_moba_prompt_kernel, ksel=ksel),
        grid=(batch, seq // ATT_TILE),
        in_specs=[pl.BlockSpec((None, ATT_TILE, w), lambda b, i: (b, i, 0)),
                  pl.BlockSpec((None, seq, w), lambda b, i: (b, 0, 0)),
                  pl.BlockSpec((None, seq, w), lambda b, i: (b, 0, 0))],
        out_specs=pl.BlockSpec((None, ATT_TILE, w), lambda b, i: (b, i, 0)),
        out_shape=jax.ShapeDtypeStruct((batch, seq, w), BF16),
        scratch_shapes=[pltpu.VMEM((seq // MOBA_BLOCK, w), F32)],
        compiler_params=_cparams(("parallel", "arbitrary")),
        name="moba_prompt",
    )(r3(mq), r3(mk), r3(mv)).reshape(batch * seq, w)


def _page_copies(pt_ref, seq_idx, layer, n_pages, slot, pairs, sem):
    copies = []
    for p in range(n_pages):
        page = pt_ref[seq_idx, p]
        for cache, buf in pairs:
            copies.append(pltpu.make_async_copy(
                cache.at[layer, page], buf.at[slot, pl.ds(p * PAGE_SIZE, PAGE_SIZE), :], sem.at[slot]))
    return copies


def _gather_pages(pt_ref, layer, n_pages, pairs, sem):
    b = pl.program_id(0)
    nb = pl.num_programs(0)
    slot = b % 2

    @pl.when(b == 0)
    def _():
        for c in _page_copies(pt_ref, b, layer, n_pages, slot, pairs, sem):
            c.start()

    @pl.when(b + 1 < nb)
    def _():
        for c in _page_copies(pt_ref, b + 1, layer, n_pages, 1 - slot, pairs, sem):
            c.start()

    for c in _page_copies(pt_ref, b, layer, n_pages, slot, pairs, sem):
        c.wait()
    return slot


def _mla_sample_kernel(pt_ref, q_ref, ckv_ref, kpep_ref, wuk_ref, wuv_ref, lat_hbm, pe_hbm, o_ref,
                       lat_buf, pe_buf, sem, *, layer, n_pages):
    slot = _gather_pages(pt_ref, layer, n_pages, [(lat_hbm, lat_buf), (pe_hbm, pe_buf)], sem)
    t = q_ref.shape[0]
    q = q_ref[...]
    qa, qp = [], []
    for h in range(MLA_HEADS):
        qa.append(_dot(q[:, h * 256:h * 256 + LANES], wuk_ref[h]))
        qp.append(q[:, h * 256 + LANES:h * 256 + LANES + MLA_ROPE].astype(F32))
    qa = jnp.concatenate(qa, axis=0).astype(BF16)
    qp = jnp.concatenate(qp, axis=0).astype(BF16)
    lat = lat_buf[slot].astype(BF16)
    pe = pe_buf[slot].astype(BF16)
    s_p = _dot_nt(qa, lat) + _dot_nt(qp, pe)
    ckv = ckv_ref[...].astype(BF16)
    kpn = kpep_ref[...][:, :MLA_ROPE]
    s_n = _dot_nt(qa, ckv) + _dot_nt(qp, kpn)
    r = lax.broadcasted_iota(jnp.int32, s_n.shape, 0) % t
    c = lax.broadcasted_iota(jnp.int32, s_n.shape, 1)
    s_n = jnp.where(c <= r, s_n, NEG_INF)
    m = jnp.maximum(jnp.max(s_p, axis=-1, keepdims=True), jnp.max(s_n, axis=-1, keepdims=True))
    p_p = jnp.exp(s_p - m)
    p_n = jnp.exp(s_n - m)
    l = jnp.sum(p_p, axis=-1, keepdims=True) + jnp.sum(p_n, axis=-1, keepdims=True)
    o_lat = (_dot(p_p.astype(BF16), lat) + _dot(p_n.astype(BF16), ckv)) / l
    for h in range(MLA_HEADS):
        o_ref[:, h * LANES:(h + 1) * LANES] = _dot(o_lat[h * t:(h + 1) * t].astype(BF16), wuv_ref[h]).astype(BF16)


def _mla_sample_call(page_table, qall_s, ckv_s, kpep_s, wuk_pad, wuv_pad, cache_lat, cache_pe, layer):
    db, n_pages = page_table.shape
    t = qall_s.shape[0] // db
    past = n_pages * PAGE_SIZE
    kvl = cache_lat.shape[-1]
    q3 = qall_s.reshape(db, t, MLA_HEADS * 256)
    c3 = ckv_s.reshape(db, t, kvl)
    k3 = kpep_s.reshape(db, t, LANES)
    seq_block = lambda w: pl.BlockSpec((None, t, w), lambda b, pt: (b, 0, 0))
    full = lambda a: pl.BlockSpec(a.shape, lambda b, pt: (0,) * a.ndim)
    grid_spec = pltpu.PrefetchScalarGridSpec(
        num_scalar_prefetch=1,
        grid=(db,),
        in_specs=[seq_block(MLA_HEADS * 256), seq_block(kvl), seq_block(LANES), full(wuk_pad), full(wuv_pad),
                  pl.BlockSpec(memory_space=pl.ANY), pl.BlockSpec(memory_space=pl.ANY)],
        out_specs=seq_block(MLA_HEADS * LANES),
        scratch_shapes=[pltpu.VMEM((2, past, kvl), F32), pltpu.VMEM((2, past, MLA_ROPE), F32),
                        pltpu.SemaphoreType.DMA((2,))])
    return pl.pallas_call(
        functools.partial(_mla_sample_kernel, layer=layer, n_pages=n_pages),
        grid_spec=grid_spec,
        out_shape=jax.ShapeDtypeStruct((db, t, MLA_HEADS * LANES), BF16),
        compiler_params=_cparams(("arbitrary",)),
        name="mla_sample",
    )(page_table, q3, c3, k3, wuk_pad, wuv_pad, cache_lat, cache_pe).reshape(db * t, MLA_HEADS * LANES)


def _moba_sample_kernel(pt_ref, q_ref, kn_ref, vn_ref, k_hbm, v_hbm, o_ref, k_buf, v_buf, sem,
                        *, layer, n_pages):
    slot = _gather_pages(pt_ref, layer, n_pages, [(k_hbm, k_buf), (v_hbm, v_buf)], sem)
    t = q_ref.shape[0]
    past = n_pages * PAGE_SIZE
    nblk = past // MOBA_BLOCK
    w = q_ref.shape[1]
    qf = _stack_heads(q_ref[...], MOBA_HEADS, MOBA_DIM)
    q = qf.astype(BF16)
    k = k_buf[slot]
    kmean = jnp.mean(k.reshape(nblk, MOBA_BLOCK, w), axis=1)
    bias = _topk_bias(_dot3_nt(qf, kmean), nblk, min(MOBA_TOPK, nblk))
    expand = (lax.broadcasted_iota(jnp.int32, (nblk, past), 1) // MOBA_BLOCK
              == lax.broadcasted_iota(jnp.int32, (nblk, past), 0)).astype(BF16)
    s_p = _dot_nt(q, k.astype(BF16)) + _dot(bias.astype(BF16), expand)
    kn = kn_ref[...].astype(BF16)
    s_n = _dot_nt(q, kn)
    r = lax.broadcasted_iota(jnp.int32, s_n.shape, 0) % t
    c = lax.broadcasted_iota(jnp.int32, s_n.shape, 1)
    s_n = jnp.where(c <= r, s_n, NEG_INF)
    m = jnp.maximum(jnp.max(s_p, axis=-1, keepdims=True), jnp.max(s_n, axis=-1, keepdims=True))
    p_p = jnp.exp(s_p - m)
    p_n = jnp.exp(s_n - m)
    l = jnp.sum(p_p, axis=-1, keepdims=True) + jnp.sum(p_n, axis=-1, keepdims=True)
    o = (_dot(p_p.astype(BF16), v_buf[slot].astype(BF16)) + _dot(p_n.astype(BF16), vn_ref[...].astype(BF16))) / l
    o_ref[...] = _unstack_heads(o, MOBA_HEADS, MOBA_DIM).astype(BF16)


def _moba_sample_call(page_table, mq_s, mk_s, mv_s, cache_k, cache_v, layer):
    db, n_pages = page_table.shape
    t = mq_s.shape[0] // db
    past = n_pages * PAGE_SIZE
    w = MOBA_HEADS * MOBA_DIM
    assert past % MOBA_BLOCK == 0 and past >= MOBA_BLOCK and t <= MOBA_BLOCK
    ck = cache_k.reshape(cache_k.shape[:3] + (w,))
    cv = cache_v.reshape(cache_v.shape[:3] + (w,))
    seq_block = pl.BlockSpec((None, t, w), lambda b, pt: (b, 0, 0))
    grid_spec = pltpu.PrefetchScalarGridSpec(
        num_scalar_prefetch=1,
        grid=(db,),
        in_specs=[seq_block, seq_block, seq_block,
                  pl.BlockSpec(memory_space=pl.ANY), pl.BlockSpec(memory_space=pl.ANY)],
        out_specs=seq_block,
        scratch_shapes=[pltpu.VMEM((2, past, w), F32), pltpu.VMEM((2, past, w), F32),
                        pltpu.SemaphoreType.DMA((2,))])
    r3 = lambda a: a.reshape(db, t, w)
    return pl.pallas_call(
        functools.partial(_moba_sample_kernel, layer=layer, n_pages=n_pages),
        grid_spec=grid_spec,
        out_shape=jax.ShapeDtypeStruct((db, t, w), BF16),
        compiler_params=_cparams(("arbitrary",)),
        name="moba_sample",
    )(page_table, r3(mq_s), r3(mk_s), r3(mv_s), ck, cv).reshape(db * t, w)


def _merge_kernel(x_ref, a_ref, b_ref, c_ref, g_ref, wpa_ref, wpb_ref, wpc_ref, wout_ref, gffn_ref, wq_ref,
                  xo_ref, hn_ref, qp_ref):
    d = x_ref.shape[1]
    m = (g_ref[:, 0:d].astype(F32) * _dot(a_ref[...], wpa_ref[...])
         + g_ref[:, d:2 * d].astype(F32) * _dot(b_ref[...], wpb_ref[...])
         + g_ref[:, 2 * d:3 * d].astype(F32) * _dot(c_ref[...], wpc_ref[...]))
    x = x_ref[...] + _dot(m.astype(BF16), wout_ref[...])
    xo_ref[...] = x
    hn = _rms(x, gffn_ref[...]).astype(BF16)
    hn_ref[...] = hn
    qp_ref[...] = _dot(hn, wq_ref[...])


def _merge_call(x_all, a_all, b_all, c_all, gates, wpa_pad, wpb, wpc, wout, g_ffn, wq):
    t, d = x_all.shape
    tm = TOK_TILE
    row = lambda w: pl.BlockSpec((tm, w), lambda i: (i, 0))
    full = lambda a: pl.BlockSpec(a.shape, lambda i: (0,) * a.ndim)
    nq = wq.shape[1]
    return pl.pallas_call(
        _merge_kernel,
        grid=(t // tm,),
        in_specs=[row(d), row(a_all.shape[1]), row(b_all.shape[1]), row(c_all.shape[1]), row(3 * d),
                  full(wpa_pad), full(wpb), full(wpc), full(wout), full(g_ffn), full(wq)],
        out_specs=[row(d), row(d), row(nq)],
        out_shape=[jax.ShapeDtypeStruct((t, d), F32), jax.ShapeDtypeStruct((t, d), BF16),
                   jax.ShapeDtypeStruct((t, nq), F32)],
        compiler_params=_cparams(("parallel",)),
        name="merge",
    )(x_all, a_all, b_all, c_all, gates, wpa_pad, wpb, wpc, wout, g_ffn, wq)


def _top16(s):
    n = s.shape[0]
    rows = lax.broadcasted_iota(jnp.int32, s.shape, 0).astype(F32)
    vals = []
    rank = jnp.full(s.shape, float(PEER_TOPK), F32)
    for r in range(PEER_TOPK):
        m = jnp.max(s, axis=0, keepdims=True)
        idx = jnp.min(jnp.where(s == m, rows, float(n)), axis=0, keepdims=True)
        hit = rows == idx
        rank = jnp.where(hit, float(r), rank)
        s = jnp.where(hit, -jnp.inf, s)
        vals.append(m)
    return jnp.concatenate(vals, axis=0), rank


def _peer_select_kernel(q_ref, keys_ref, o_ref):
    q = q_ref[...]
    half = q.shape[1] // 2
    s0 = _dot3_nt(keys_ref[0], q[:, :half])
    s1 = _dot3_nt(keys_ref[1], q[:, half:])
    sv0, rank0 = _top16(s0)
    sv1, rank1 = _top16(s1)
    k = PEER_TOPK
    tl = q.shape[0]
    cand = (sv0[:, None, :] + sv1[None, :, :]).reshape(k * k, tl)
    rows = lax.broadcasted_iota(jnp.int32, cand.shape, 0).astype(F32)
    picked = jnp.zeros(cand.shape, F32)
    z = jnp.zeros((1, tl), F32)
    top = None
    for r in range(k):
        m = jnp.max(cand, axis=0, keepdims=True)
        idx = jnp.min(jnp.where(cand == m, rows, float(k * k)), axis=0, keepdims=True)
        hit = rows == idx
        picked = jnp.where(hit, 1.0, picked)
        cand = jnp.where(hit, -jnp.inf, cand)
        if r == 0:
            top = m
        z = z + jnp.exp(m - top)
    cnt = jnp.sum(picked.reshape(k, k, tl), axis=1)
    count = jnp.zeros(s0.shape, F32)
    for r in range(k):
        count = count + jnp.where(rank0 == float(r), cnt[r:r + 1, :], 0.0)
    o_ref[0] = rank1
    o_ref[1] = count
    o_ref[2] = jnp.exp(s0 - sv0[0:1, :])
    o_ref[3] = jnp.exp(s1 - sv1[0:1, :]) / z


def _peer_select_call(qp, keys):
    t = qp.shape[0]
    heads, _, n_keys, half = keys.shape
    tl = SEL_TILE
    return pl.pallas_call(
        _peer_select_kernel,
        grid=(t // tl, heads),
        in_specs=[pl.BlockSpec((tl, 2 * half), lambda i, h: (i, h)),
                  pl.BlockSpec((None, 2, n_keys, half), lambda i, h: (h, 0, 0, 0))],
        out_specs=pl.BlockSpec((None, 4, n_keys, tl), lambda i, h: (h, 0, 0, i)),
        out_shape=jax.ShapeDtypeStruct((heads, 4, n_keys, t), F32),
        compiler_params=_cparams(("parallel", "arbitrary")),
        name="peer_select",
    )(qp, keys)


def _peer_expert_kernel(hn_ref, x_ref, sel_ref, u_ref, vt_ref, gfin_ref, o_ref, acc_ref, *, final_norm):
    e = pl.program_id(1)
    n_keys = sel_ref.shape[2]
    heads = sel_ref.shape[0]

    @pl.when(e == 0)
    def _():
        acc_ref[...] = jnp.zeros_like(acc_ref)

    act = jax.nn.gelu(_dot_nt(u_ref[...].astype(BF16), hn_ref[...]))
    parts = []
    for ii in range(PEER_ROWS):
        i = e * PEER_ROWS + ii
        g = jnp.zeros((n_keys, act.shape[1]), F32)
        for h in range(heads):
            cnt = sel_ref[h, 1, pl.ds(i, 1), :]
            a = sel_ref[h, 2, pl.ds(i, 1), :]
            g = g + jnp.where(sel_ref[h, 0] < cnt, a * sel_ref[h, 3], 0.0)
        parts.append((g * act[ii * n_keys:(ii + 1) * n_keys]).astype(BF16))
    wt = jnp.concatenate(parts, axis=0)
    acc_ref[...] += _dot(vt_ref[...].astype(BF16), wt)

    @pl.when(e == pl.num_programs(1) - 1)
    def _():
        x = x_ref[...] + acc_ref[...].T
        if final_norm:
            x = _rms(x, gfin_ref[...])
        o_ref[...] = x


def _peer_expert_call(hn, x_all, sel, u_tab, v_tab_t, g_final, final_norm):
    t, d = x_all.shape
    heads, _, n_keys, _ = sel.shape
    tt = PEER_TOK_TILE
    te = PEER_ROWS * n_keys
    return pl.pallas_call(
        functools.partial(_peer_expert_kernel, final_norm=final_norm),
        grid=(t // tt, n_keys // PEER_ROWS),
        in_specs=[pl.BlockSpec((tt, d), lambda i, e: (i, 0)),
                  pl.BlockSpec((tt, d), lambda i, e: (i, 0)),
                  pl.BlockSpec((heads, 4, n_keys, tt), lambda i, e: (0, 0, 0, i)),
                  pl.BlockSpec((te, d), lambda i, e: (e, 0)),
                  pl.BlockSpec((d, te), lambda i, e: (0, e)),
                  pl.BlockSpec((1, d), lambda i, e: (0, 0))],
        out_specs=pl.BlockSpec((tt, d), lambda i, e: (i, 0)),
        out_shape=jax.ShapeDtypeStruct((t, d), F32),
        scratch_shapes=[pltpu.VMEM((d, tt), F32)],
        compiler_params=_cparams(("parallel", "arbitrary")),
        name="peer_expert",
    )(hn, x_all, sel, u_tab, v_tab_t, g_final)


def kernel(x_prompt, x_sample, cache_mla_latent, cache_mla_krope, cache_moba_k, cache_moba_v, page_table,
           g_mix, w_in, g_q_lat, w_uq, g_kv_lat, w_ukv, g_gm_v, b_gm_v, w_s, b_s,
           w_pa, w_pb, w_pc, w_out, g_ffn, w_peer_q, peer_keys, peer_u, peer_v, g_final):
    batch, seq, d = x_prompt.shape
    db, t_new, _ = x_sample.shape
    depth = w_in.shape[0]
    n_pages = page_table.shape[1]
    past = n_pages * PAGE_SIZE
    n_p, n_s = batch * seq, db * t_new
    assert n_p % TOK_TILE == 0 and n_s % TOK_TILE == 0 and GM_CHUNK % t_new == 0
    assert (n_p + n_s) % PEER_TOK_TILE == 0 and seq % ATT_TILE == 0

    pos = jnp.concatenate([jnp.tile(jnp.arange(seq, dtype=jnp.int32), batch),
                           jnp.tile(past + jnp.arange(t_new, dtype=jnp.int32), db)])
    tabs = _rope_tables(pos)
    x_all = jnp.concatenate([x_prompt.reshape(n_p, d), x_sample.reshape(n_s, d)], axis=0)
    row2 = lambda a: a.reshape(1, -1)
    kvl = w_ukv.shape[1]

    outs = {k: [] for k in ("lat", "pe", "mk", "mv", "gv")}
    for l in range(depth):
        w_arr = _arrange_w_in(w_in[l])
        wuq, wuqr = _arrange_w_uq(w_uq[l])
        reps = GM_CHUNK // t_new
        w_small = w_s[l][:, :t_new, :t_new]
        eye = jnp.eye(reps, dtype=w_s.dtype)
        w_samp = jnp.einsum("ab,gts->gatbs", eye, w_small).reshape(GM_GROUPS, GM_CHUNK, GM_CHUNK)
        wmix = jnp.stack([w_s[l], w_samp])
        b_full = jnp.repeat(b_s[l].T, GM_DIM, axis=1)
        b_samp = jnp.tile(jnp.repeat(b_s[l][:, :t_new].T, GM_DIM, axis=1), (reps, 1))
        bmix = jnp.stack([b_full, b_samp])
        w3 = w_ukv[l].reshape(w_ukv.shape[1], MLA_HEADS, MLA_NOPE + MLA_V)
        wuk_pad = jnp.concatenate(
            [jnp.transpose(w3[..., :MLA_NOPE], (1, 2, 0)),
             jnp.zeros((MLA_HEADS, LANES - MLA_NOPE, w3.shape[0]), w3.dtype)], axis=1).astype(BF16)
        wuv_pad = jnp.concatenate(
            [jnp.zeros((MLA_HEADS, w3.shape[0], LANES - MLA_V), w3.dtype),
             jnp.transpose(w3[..., MLA_NOPE:], (1, 0, 2))], axis=2).astype(BF16)
        wpa3 = w_pa[l].reshape(MLA_HEADS, MLA_V, d)
        wpa_pad = jnp.concatenate([jnp.zeros((MLA_HEADS, LANES - MLA_V, d), w_pa.dtype), wpa3],
                                  axis=1).reshape(MLA_HEADS * LANES, d).astype(BF16)

        (ckv, kpe, mk, mv, gv, qall, kvx, kpep, mq, c_all, gates) = _proj_call(
            x_all, tabs, n_p // TOK_TILE, row2(g_mix[l]), w_arr, row2(g_q_lat[l]), wuq, wuqr,
            row2(g_kv_lat[l]), w_ukv[l].astype(BF16), row2(g_gm_v[l]), row2(b_gm_v[l]), wmix, bmix)

        a_p = _mla_prompt_call(qall, kvx, kpep, batch, seq)
        b_p = _moba_prompt_call(mq, mk, mv, batch, seq)
        a_s = _mla_sample_call(page_table, qall[n_p:], ckv[n_p:], kpep[n_p:], wuk_pad, wuv_pad,
                               cache_mla_latent, cache_mla_krope, l)
        b_s_out = _moba_sample_call(page_table, mq[n_p:], mk[n_p:], mv[n_p:], cache_moba_k, cache_moba_v, l)
        a_all = jnp.concatenate([a_p, a_s], axis=0)
        b_all = jnp.concatenate([b_p, b_s_out], axis=0)

        x_mid, hn, qp = _merge_call(x_all, a_all, b_all, c_all, gates, wpa_pad, w_pb[l].astype(BF16),
                                    w_pc[l].astype(BF16), w_out[l].astype(BF16), row2(g_ffn[l]),
                                    w_peer_q[l].astype(BF16))
        sel = _peer_select_call(qp, peer_keys[l])
        x_all = _peer_expert_call(hn, x_mid, sel, peer_u[l], peer_v[l].T, row2(g_final), l == depth - 1)

        outs["lat"].append(ckv)
        outs["pe"].append(kpe)
        outs["mk"].append(mk)
        outs["mv"].append(mv)
        outs["gv"].append(gv)

    def split(name, tail):
        st = jnp.stack(outs[name])
        return (st[:, :n_p].reshape((depth, batch, seq) + tail), st[:, n_p:].reshape((depth, db, t_new) + tail))

    lat_p, lat_s = split("lat", (kvl,))
    pe_p, pe_s = split("pe", (MLA_ROPE,))
    mk_p, mk_s = split("mk", (MOBA_HEADS, MOBA_DIM))
    mv_p, mv_s = split("mv", (MOBA_HEADS, MOBA_DIM))
    _, gv_s = split("gv", (GM_GROUPS, GM_DIM))
    y_prompt = x_all[:n_p].reshape(batch, seq, d)
    y_sample = x_all[n_p:].reshape(db, t_new, d)
    return (y_prompt, y_sample, lat_p, pe_p, mk_p, mv_p, lat_s, pe_s, mk_s, mv_s, gv_s)
```

```python
import functools

import jax
import jax.numpy as jnp
import numpy as np
from jax import lax
from jax.experimental import pallas as pl
from jax.experimental.pallas import tpu as pltpu

F32 = jnp.float32
BF16 = jnp.bfloat16

MLA_HEADS = 8
MLA_NOPE = 64
MLA_ROPE = 32
MLA_V = 64
MOBA_HEADS = 4
MOBA_DIM = 64
MOBA_BLOCK = 256
MOBA_TOPK = 3
GM_GROUPS = 4
GM_DIM = 64
GM_CHUNK = 128
PEER_HEADS = 8
PEER_TOPK = 16
PAGE_SIZE = 128
ROPE_THETA = 10000.0
EPS = 1e-6
NEG_INF = -1e30

LANES = 128
_BF16_SUBLANES = 16
VMEM_LIMIT = 56 * 1024 * 1024

TOK_TILE = 256
ATT_TILE = 256
MLA_HEAD_GROUP = 8
SEL_TILE = 256
PEER_TOK_TILE = 512
PEER_ROWS = 4

_NT = (((1,), (1,)), ((), ()))


def _cparams(sem):
    return pltpu.CompilerParams(dimension_semantics=sem, vmem_limit_bytes=VMEM_LIMIT)


def _dot(a, b):
    return jnp.dot(a, b, preferred_element_type=F32)


def _dot_nt(a, b):
    return lax.dot_general(a, b, _NT, preferred_element_type=F32)


def _split(a):
    hi = a.astype(BF16)
    lo = (a - hi.astype(F32)).astype(BF16)
    return hi, lo


def _dot3_nt(a, b):
    ah, al = _split(a)
    bh, bl = _split(b)
    return _dot_nt(ah, bh) + _dot_nt(ah, bl) + _dot_nt(al, bh)


def _rms(x, g):
    return x * lax.rsqrt(jnp.mean(x * x, axis=-1, keepdims=True) + EPS) * g


def _gelu_tanh(x):
    c = float(np.sqrt(2.0 / np.pi))
    u = x * ((x * x) * (c * 0.044715) + c)
    hx = 0.5 * x
    return hx * jnp.tanh(u) + hx


def _lane_group(shape, width):
    return lax.broadcasted_iota(jnp.int32, shape, len(shape) - 1) // width


_C_QLAT, _C_KVLAT, _C_KPE, _C_KPER = 0, 256, 512, 640
_C_MQ, _C_MQR, _C_MK, _C_MKR, _C_MV, _C_U, _C_V, _C_G = 768, 1024, 1280, 1536, 1792, 2048, 2304, 2560


def _proj_kernel(x_ref, gmix_ref, w_ref, gq_ref, wuq_ref, wuqr_ref, gkv_ref, wukv_ref,
                 ggv_ref, bgv_ref, wmix_ref, bmix_ref,
                 cosq_ref, sinq_ref, cosk_ref, sink_ref, cosm_ref, sinm_ref,
                 ckv_ref, kpe_ref, mk_ref, mv_ref, gv_ref,
                 qall_ref, kvx_ref, kpep_ref, mq_ref, c_ref, gates_ref):
    d_model = x_ref.shape[1]
    hb = _rms(x_ref[...], gmix_ref[...]).astype(BF16)

    def seg(off, width):
        return _dot(hb, w_ref[:, off:off + width])

    qn = _rms(seg(_C_QLAT, 256), gq_ref[...]).astype(BF16)
    cq, sq = cosq_ref[...], sinq_ref[...]
    for h in range(MLA_HEADS):
        sl = slice(h * 256, (h + 1) * 256)
        qa = _dot(qn, wuq_ref[:, sl])
        qr = _dot(qn, wuqr_ref[:, sl])
        qall_ref[:, sl] = (qa * cq + qr * sq).astype(BF16)

    ckv = _rms(seg(_C_KVLAT, 256), gkv_ref[...])
    ckv_ref[...] = ckv
    kvx_ref[...] = _dot(ckv.astype(BF16), wukv_ref[...]).astype(BF16)

    kpe = seg(_C_KPE, LANES) * cosk_ref[...] + seg(_C_KPER, LANES) * sink_ref[...]
    kpep_ref[...] = kpe.astype(BF16)
    kpe_ref[...] = kpe[:, :MLA_ROPE]

    cm, sm = cosm_ref[...], sinm_ref[...]
    mq_ref[...] = seg(_C_MQ, 256) * cm[:, :256] + seg(_C_MQR, 256) * sm[:, :256]
    mk_ref[...] = seg(_C_MK, 256) * cm[:, 256:] + seg(_C_MKR, 256) * sm[:, 256:]
    mv_ref[...] = seg(_C_MV, 256)

    u = jax.nn.gelu(seg(_C_U, 256))
    gvx = jax.nn.gelu(seg(_C_V, 256))
    mu = jnp.mean(gvx, axis=-1, keepdims=True)
    xc = gvx - mu
    v = xc * lax.rsqrt(jnp.mean(xc * xc, axis=-1, keepdims=True) + EPS) * ggv_ref[...] + bgv_ref[...]
    gv_ref[...] = v
    tm = x_ref.shape[0]
    row = lax.broadcasted_iota(jnp.int32, (GM_CHUNK, GM_CHUNK), 0)
    col = lax.broadcasted_iota(jnp.int32, (GM_CHUNK, GM_CHUNK), 1)
    grp = _lane_group((GM_CHUNK, GM_GROUPS * GM_DIM), GM_DIM)
    for ci in range(tm // GM_CHUNK):
        rs = slice(ci * GM_CHUNK, (ci + 1) * GM_CHUNK)
        vc = v[rs]
        mixed = bmix_ref[...]
        for g in range(GM_GROUPS):
            wg = jnp.where(col <= row, wmix_ref[g], 0.0).astype(BF16)
            vg = jnp.where(grp == g, vc, 0.0).astype(BF16)
            mixed = mixed + _dot(wg, vg)
        c_ref[rs, :] = (u[rs] * mixed).astype(BF16)

    for j in range(3):
        gates_ref[:, j * d_model:(j + 1) * d_model] = jax.nn.sigmoid(
            seg(_C_G + j * d_model, d_model)).astype(BF16)


def _rot_cols(w, d):
    k, n = w.shape
    w3 = w.reshape(k, n // d, d)
    return jnp.concatenate([-w3[..., d // 2:], w3[..., :d // 2]], axis=-1).reshape(k, n)


def _rope_tables(pos):
    pos = pos.astype(F32)[:, None]

    def cs(d):
        inv = ROPE_THETA ** (-jnp.arange(0, d, 2, dtype=F32) / d)
        ang = pos * inv[None, :]
        return (jnp.concatenate([jnp.cos(ang)] * 2, axis=1), jnp.concatenate([jnp.sin(ang)] * 2, axis=1))

    n = pos.shape[0]
    c32, s32 = cs(MLA_ROPE)
    c64, s64 = cs(MOBA_DIM)
    ones, zeros = jnp.ones((n, LANES), F32), jnp.zeros((n, LANES), F32)
    pad = jnp.zeros((n, LANES - MLA_ROPE), F32)
    mla_scale = (MLA_NOPE + MLA_ROPE) ** -0.5
    cosq = jnp.concatenate([ones, c32, pad], axis=1) * mla_scale
    sinq = jnp.concatenate([zeros, s32, pad], axis=1) * mla_scale
    cosk = jnp.concatenate([c32, pad], axis=1)
    sink = jnp.concatenate([s32, pad], axis=1)
    moba_scale = MOBA_DIM ** -0.5
    c64h, s64h = jnp.tile(c64, (1, MOBA_HEADS)), jnp.tile(s64, (1, MOBA_HEADS))
    cosm = jnp.concatenate([c64h * moba_scale, c64h], axis=1)
    sinm = jnp.concatenate([s64h * moba_scale, s64h], axis=1)
    return cosq, sinq, cosk, sink, cosm, sinm


def _arrange_w_in(w_in):
    d = w_in.shape[0]
    o = np.cumsum([0, 256, 256, MLA_ROPE, 256, 256, 256, 256, 256])
    q_lat, kv_lat, k_rope, m_q, m_k, m_v, g_u, g_v = (w_in[:, o[i]:o[i + 1]] for i in range(8))
    gates = w_in[:, o[8]:]
    padk = jnp.zeros((d, LANES - MLA_ROPE), w_in.dtype)
    cols = [q_lat, kv_lat, k_rope, padk, _rot_cols(k_rope, MLA_ROPE), padk,
            m_q, _rot_cols(m_q, MOBA_DIM), m_k, _rot_cols(m_k, MOBA_DIM), m_v, g_u, g_v, gates]
    return jnp.concatenate(cols, axis=1).astype(BF16)


def _arrange_w_uq(w_uq):
    k = w_uq.shape[0]
    w3 = w_uq.reshape(k, MLA_HEADS, MLA_NOPE + MLA_ROPE)
    nope, pe = w3[..., :MLA_NOPE], w3[..., MLA_NOPE:]
    z64 = jnp.zeros((k, MLA_HEADS, LANES - MLA_NOPE), w_uq.dtype)
    z96 = jnp.zeros((k, MLA_HEADS, LANES - MLA_ROPE), w_uq.dtype)
    z128 = jnp.zeros((k, MLA_HEADS, LANES), w_uq.dtype)
    pe_rot = jnp.concatenate([-pe[..., MLA_ROPE // 2:], pe[..., :MLA_ROPE // 2]], axis=-1)
    big = jnp.concatenate([nope, z64, pe, z96], axis=-1).reshape(k, MLA_HEADS * 256)
    big_rot = jnp.concatenate([z128, pe_rot, z96], axis=-1).reshape(k, MLA_HEADS * 256)
    return big.astype(BF16), big_rot.astype(BF16)


def _proj_call(x_all, tabs, n_prompt_tiles, g_mix, w_arr, g_q, wuq, wuqr, g_kv, w_ukv, g_gv, b_gv, wmix, bmix):
    t, d = x_all.shape
    tm = TOK_TILE
    nt = t // tm
    row = lambda w: pl.BlockSpec((tm, w), lambda i: (i, 0))
    full = lambda a: pl.BlockSpec(a.shape, lambda i: (0,) * a.ndim)
    kind = lambda i: jnp.where(i >= n_prompt_tiles, 1, 0)
    in_specs = [row(d), full(g_mix), full(w_arr), full(g_q), full(wuq), full(wuqr), full(g_kv), full(w_ukv),
                full(g_gv), full(b_gv),
                pl.BlockSpec((None, GM_GROUPS, GM_CHUNK, GM_CHUNK), lambda i: (kind(i), 0, 0, 0)),
                pl.BlockSpec((None, GM_CHUNK, GM_GROUPS * GM_DIM), lambda i: (kind(i), 0, 0)),
                row(256), row(256), row(LANES), row(LANES), row(512), row(512)]
    outs = [((t, 256), F32), ((t, MLA_ROPE), F32), ((t, 256), F32), ((t, 256), F32), ((t, 256), F32),
            ((t, MLA_HEADS * 256), BF16), ((t, MLA_HEADS * 128), BF16), ((t, LANES), BF16),
            ((t, 256), F32), ((t, 256), BF16), ((t, 3 * d), BF16)]
    return pl.pallas_call(
        _proj_kernel,
        grid=(nt,),
        in_specs=in_specs,
        out_specs=[row(s[1]) for s, _ in outs],
        out_shape=[jax.ShapeDtypeStruct(s, dt) for s, dt in outs],
        compiler_params=_cparams(("parallel",)),
        name="proj",
    )(x_all, g_mix, w_arr, g_q, wuq, wuqr, g_kv, w_ukv, g_gv, b_gv, wmix, bmix, *tabs)


def _mla_prompt_kernel(q_ref, kv_ref, kpe_ref, o_ref):
    tq = q_ref.shape[0]
    qi = pl.program_id(2)
    hg = MLA_HEAD_GROUP

    def tile(j, carry, masked):
        rows = pl.ds(pl.multiple_of(j * tq, tq), tq)
        kpe = kpe_ref[rows, :]
        out = []
        for h in range(hg):
            m, l, acc = carry[h]
            kv = kv_ref[rows, h * LANES:(h + 1) * LANES]
            s = _dot_nt(q_ref[:, h * 256:(h + 1) * 256], jnp.concatenate([kv, kpe], axis=1))
            if masked:
                r = lax.broadcasted_iota(jnp.int32, s.shape, 0)
                c = lax.broadcasted_iota(jnp.int32, s.shape, 1)
                s = jnp.where(c <= r, s, NEG_INF)
            m_new = jnp.maximum(m, jnp.max(s, axis=-1, keepdims=True))
            alpha = jnp.exp(m - m_new)
            p = jnp.exp(s - m_new)
            l = alpha * l + jnp.sum(p, axis=-1, keepdims=True)
            acc = alpha * acc + _dot(p.astype(BF16), kv)
            out.append((m_new, l, acc))
        return tuple(out)

    init = tuple((jnp.full((tq, 1), NEG_INF, F32), jnp.zeros((tq, 1), F32), jnp.zeros((tq, LANES), F32))
                 for _ in range(hg))
    carry = tile(qi, init, True)
    carry = lax.fori_loop(0, qi, lambda j, c: tile(j, c, False), carry)
    for h in range(hg):
        m, l, acc = carry[h]
        o_ref[:, h * LANES:(h + 1) * LANES] = (acc / l).astype(BF16)


def _mla_prompt_call(qall, kvx, kpep, batch, seq):
    tq = ATT_TILE
    hg = MLA_HEAD_GROUP
    q3 = qall[:batch * seq].reshape(batch, seq, MLA_HEADS * 256)
    kv3 = kvx[:batch * seq].reshape(batch, seq, MLA_HEADS * LANES)
    kp3 = kpep[:batch * seq].reshape(batch, seq, LANES)
    return pl.pallas_call(
        _mla_prompt_kernel,
        grid=(batch, MLA_HEADS // hg, seq // tq),
        in_specs=[pl.BlockSpec((None, tq, hg * 256), lambda b, h, i: (b, i, h)),
                  pl.BlockSpec((None, seq, hg * LANES), lambda b, h, i: (b, 0, h)),
                  pl.BlockSpec((None, seq, LANES), lambda b, h, i: (b, 0, 0))],
        out_specs=pl.BlockSpec((None, tq, hg * LANES), lambda b, h, i: (b, i, h)),
        out_shape=jax.ShapeDtypeStruct((batch, seq, MLA_HEADS * LANES), BF16),
        compiler_params=_cparams(("parallel", "parallel", "arbitrary")),
        name="mla_prompt",
    )(q3, kv3, kp3).reshape(batch * seq, MLA_HEADS * LANES)


def _topk_bias(gate, n_valid, k):
    nblk = gate.shape[1]
    blk = lax.broadcasted_iota(jnp.int32, gate.shape, 1)
    valid = blk < n_valid
    g = jnp.where(valid, gate, NEG_INF)
    rank = jnp.zeros(gate.shape, F32)
    for n in range(nblk):
        gn = g[:, n:n + 1]
        ahead = (gn > g) | ((gn == g) & (n < blk))
        rank = rank + jnp.where(ahead, 1.0, 0.0)
    return jnp.where(valid & (rank < k), 0.0, NEG_INF)


def _stack_heads(q, heads, width):
    grp = _lane_group(q.shape, width)
    return jnp.concatenate([jnp.where(grp == h, q, jnp.zeros_like(q)) for h in range(heads)], axis=0)


def _unstack_heads(o, heads, width):
    rows = o.shape[0] // heads
    grp = _lane_group((rows, o.shape[1]), width)
    out = jnp.zeros((rows, o.shape[1]), F32)
    for h in range(heads):
        out = out + jnp.where(grp == h, o[h * rows:(h + 1) * rows], 0.0)
    return out


def _moba_prompt_kernel(q_ref, k_ref, v_ref, o_ref, kmean_ref, *, ksel):
    tq = q_ref.shape[0]
    nblk = k_ref.shape[0] // MOBA_BLOCK
    qi = pl.program_id(1)

    @pl.when(qi == 0)
    def _():
        for n in range(nblk):
            kmean_ref[n:n + 1, :] = jnp.mean(k_ref[n * MOBA_BLOCK:(n + 1) * MOBA_BLOCK, :], axis=0, keepdims=True)

    qf = _stack_heads(q_ref[...], MOBA_HEADS, MOBA_DIM)
    q = qf.astype(BF16)
    bias = _topk_bias(_dot3_nt(qf, kmean_ref[...]), qi, ksel)
    blk = lax.broadcasted_iota(jnp.int32, bias.shape, 1)

    def tile(j, carry, own):
        m, l, acc = carry
        rows = pl.ds(pl.multiple_of(j * MOBA_BLOCK, MOBA_BLOCK), MOBA_BLOCK)
        s = _dot_nt(q, k_ref[rows, :].astype(BF16))
        if own:
            r = lax.broadcasted_iota(jnp.int32, s.shape, 0) % tq
            c = lax.broadcasted_iota(jnp.int32, s.shape, 1)
            s = jnp.where(c <= r, s, NEG_INF)
        else:
            s = s + jnp.sum(jnp.where(blk == j, bias, 0.0), axis=1, keepdims=True)
        m_new = jnp.maximum(m, jnp.max(s, axis=-1, keepdims=True))
        alpha = jnp.exp(m - m_new)
        p = jnp.exp(s - m_new)
        l = alpha * l + jnp.sum(p, axis=-1, keepdims=True)
        acc = alpha * acc + _dot(p.astype(BF16), v_ref[rows, :].astype(BF16))
        return m_new, l, acc

    n = qf.shape[0]
    init = (jnp.full((n, 1), NEG_INF, F32), jnp.zeros((n, 1), F32), jnp.zeros((n, qf.shape[1]), F32))
    carry = tile(qi, init, True)
    m, l, acc = lax.fori_loop(0, qi, lambda j, c: tile(j, c, False), carry)
    o_ref[...] = _unstack_heads(acc / l, MOBA_HEADS, MOBA_DIM).astype(BF16)


def _moba_prompt_call(mq, mk, mv, batch, seq):
    assert ATT_TILE == MOBA_BLOCK and seq % MOBA_BLOCK == 0
    w = MOBA_HEADS * MOBA_DIM
    r3 = lambda a: a[:batch * seq].reshape(batch, seq, w)
    ksel = min(MOBA_TOPK, (seq - 1) // MOBA_BLOCK)
    return pl.pallas_call(
        functools.partial(_moba_prompt_kernel, ksel=ksel),
        grid=(batch, seq // ATT_TILE),
        in_specs=[pl.BlockSpec((None, ATT_TILE, w), lambda b, i: (b, i, 0)),
                  pl.BlockSpec((None, seq, w), lambda b, i: (b, 0, 0)),
                  pl.BlockSpec((None, seq, w), lambda b, i: (b, 0, 0))],
        out_specs=pl.BlockSpec((None, ATT_TILE, w), lambda b, i: (b, i, 0)),
        out_shape=jax.ShapeDtypeStruct((batch, seq, w), BF16),
        scratch_shapes=[pltpu.VMEM((seq // MOBA_BLOCK, w), F32)],
        compiler_params=_cparams(("parallel", "arbitrary")),
        name="moba_prompt",
    )(r3(mq), r3(mk), r3(mv)).reshape(batch * seq, w)


def _page_copies(pt_ref, seq_idx, layer, n_pages, slot, pairs, sem):
    copies = []
    for p in range(n_pages):
        page = pt_ref[seq_idx, p]
        rows = pl.ds(p * PAGE_SIZE, PAGE_SIZE)
        for cache, buf, rows_last in pairs:
            mid = (slice(None),) * (len(buf.shape) - 2)
            dst = buf.at[(slot,) + mid + (rows,)] if rows_last else buf.at[slot, rows, :]
            copies.append(pltpu.make_async_copy(cache.at[layer, page], dst, sem.at[slot]))
    return copies


def _gather_pages(pt_ref, layer, n_pages, pairs, sem):
    b = pl.program_id(0)
    nb = pl.num_programs(0)
    slot = b % 2

    @pl.when(b == 0)
    def _():
        for c in _page_copies(pt_ref, b, layer, n_pages, slot, pairs, sem):
            c.start()

    @pl.when(b + 1 < nb)
    def _():
        for c in _page_copies(pt_ref, b + 1, layer, n_pages, 1 - slot, pairs, sem):
            c.start()

    for c in _page_copies(pt_ref, b, layer, n_pages, slot, pairs, sem):
        c.wait()
    return slot


def _mla_sample_kernel(pt_ref, q_ref, ckv_ref, kpep_ref, wuk_ref, wuv_ref, lat_hbm, pe_hbm, o_ref,
                       lat_buf, pe_buf, sem, *, layer, n_pages):
    slot = _gather_pages(pt_ref, layer, n_pages, [(lat_hbm, lat_buf, False), (pe_hbm, pe_buf, True)], sem)
    t = q_ref.shape[0]
    q = q_ref[...]
    qa, qp = [], []
    for h in range(MLA_HEADS):
        qa.append(_dot(q[:, h * 256:h * 256 + LANES], wuk_ref[h]))
        qp.append(q[:, h * 256 + LANES:h * 256 + LANES + MLA_ROPE].astype(F32))
    qa = jnp.concatenate(qa, axis=0).astype(BF16)
    qp = jnp.concatenate(qp, axis=0).astype(BF16)
    lat = lat_buf[slot].astype(BF16)
    pe_t = pe_buf[slot].astype(BF16)
    s_p = _dot_nt(qa, lat) + _dot(qp, pe_t)
    ckv = ckv_ref[...].astype(BF16)
    kpn = kpep_ref[...][:, :MLA_ROPE]
    s_n = _dot_nt(qa, ckv) + _dot_nt(qp, kpn)
    r = lax.broadcasted_iota(jnp.int32, s_n.shape, 0) % t
    c = lax.broadcasted_iota(jnp.int32, s_n.shape, 1)
    s_n = jnp.where(c <= r, s_n, NEG_INF)
    m = jnp.maximum(jnp.max(s_p, axis=-1, keepdims=True), jnp.max(s_n, axis=-1, keepdims=True))
    p_p = jnp.exp(s_p - m)
    p_n = jnp.exp(s_n - m)
    l = jnp.sum(p_p, axis=-1, keepdims=True) + jnp.sum(p_n, axis=-1, keepdims=True)
    o_lat = (_dot(p_p.astype(BF16), lat) + _dot(p_n.astype(BF16), ckv)) / l
    for h in range(MLA_HEADS):
        o_ref[:, h * LANES:(h + 1) * LANES] = _dot(o_lat[h * t:(h + 1) * t].astype(BF16), wuv_ref[h]).astype(BF16)


def _mla_sample_call(page_table, qall_s, ckv_s, kpep_s, wuk_pad, wuv_pad, cache_lat, cache_pe, layer):
    db, n_pages = page_table.shape
    t = qall_s.shape[0] // db
    past = n_pages * PAGE_SIZE
    kvl = cache_lat.shape[-1]
    q3 = qall_s.reshape(db, t, MLA_HEADS * 256)
    c3 = ckv_s.reshape(db, t, kvl)
    k3 = kpep_s.reshape(db, t, LANES)
    seq_block = lambda w: pl.BlockSpec((None, t, w), lambda b, pt: (b, 0, 0))
    full = lambda a: pl.BlockSpec(a.shape, lambda b, pt: (0,) * a.ndim)
    grid_spec = pltpu.PrefetchScalarGridSpec(
        num_scalar_prefetch=1,
        grid=(db,),
        in_specs=[seq_block(MLA_HEADS * 256), seq_block(kvl), seq_block(LANES), full(wuk_pad), full(wuv_pad),
                  pl.BlockSpec(memory_space=pl.ANY), pl.BlockSpec(memory_space=pl.ANY)],
        out_specs=seq_block(MLA_HEADS * LANES),
        scratch_shapes=[pltpu.VMEM((2, past, kvl), F32), pltpu.VMEM((2, MLA_ROPE, past), F32),
                        pltpu.SemaphoreType.DMA((2,))])
    return pl.pallas_call(
        functools.partial(_mla_sample_kernel, layer=layer, n_pages=n_pages),
        grid_spec=grid_spec,
        out_shape=jax.ShapeDtypeStruct((db, t, MLA_HEADS * LANES), BF16),
        compiler_params=_cparams(("arbitrary",)),
        name="mla_sample",
    )(page_table, q3, c3, k3, wuk_pad, wuv_pad, cache_lat, cache_pe).reshape(db * t, MLA_HEADS * LANES)


def _moba_sample_kernel(pt_ref, q_ref, kn_ref, vn_ref, k_hbm, v_hbm, o_ref, k_buf, v_buf, sem,
                        *, layer, n_pages):
    slot = _gather_pages(pt_ref, layer, n_pages, [(k_hbm, k_buf, True), (v_hbm, v_buf, True)], sem)
    t = q_ref.shape[0]
    past = n_pages * PAGE_SIZE
    nblk = past // MOBA_BLOCK
    w = q_ref.shape[1]
    q = _stack_heads(q_ref[...], MOBA_HEADS, MOBA_DIM).astype(BF16)
    k_t = k_buf[slot].reshape(w, past).astype(BF16)
    s_raw = _dot(q, k_t)
    blk = lax.broadcasted_iota(jnp.int32, (q.shape[0], nblk), 1)
    gate = jnp.zeros((q.shape[0], nblk), F32)
    for n in range(nblk):
        gn = jnp.sum(s_raw[:, n * MOBA_BLOCK:(n + 1) * MOBA_BLOCK], axis=1, keepdims=True)
        gate = gate + jnp.where(blk == n, gn, 0.0)
    bias = _topk_bias(gate, nblk, min(MOBA_TOPK, nblk))
    expand = (lax.broadcasted_iota(jnp.int32, (nblk, past), 1) // MOBA_BLOCK
              == lax.broadcasted_iota(jnp.int32, (nblk, past), 0)).astype(BF16)
    s_p = s_raw + _dot(bias.astype(BF16), expand)
    kn = kn_ref[...].astype(BF16)
    s_n = _dot_nt(q, kn)
    r = lax.broadcasted_iota(jnp.int32, s_n.shape, 0) % t
    c = lax.broadcasted_iota(jnp.int32, s_n.shape, 1)
    s_n = jnp.where(c <= r, s_n, NEG_INF)
    m = jnp.maximum(jnp.max(s_p, axis=-1, keepdims=True), jnp.max(s_n, axis=-1, keepdims=True))
    p_p = jnp.exp(s_p - m)
    p_n = jnp.exp(s_n - m)
    l = jnp.sum(p_p, axis=-1, keepdims=True) + jnp.sum(p_n, axis=-1, keepdims=True)
    v_t = v_buf[slot].reshape(w, past).astype(BF16)
    o = (_dot_nt(p_p.astype(BF16), v_t) + _dot(p_n.astype(BF16), vn_ref[...].astype(BF16))) / l
    o_ref[...] = _unstack_heads(o, MOBA_HEADS, MOBA_DIM).astype(BF16)


def _moba_sample_call(page_table, mq_s, mk_s, mv_s, cache_k, cache_v, layer):
    db, n_pages = page_table.shape
    t = mq_s.shape[0] // db
    past = n_pages * PAGE_SIZE
    w = MOBA_HEADS * MOBA_DIM
    assert past % MOBA_BLOCK == 0 and past >= MOBA_BLOCK and t <= MOBA_BLOCK
    ck = jnp.transpose(cache_k, (0, 1, 3, 4, 2))
    cv = jnp.transpose(cache_v, (0, 1, 3, 4, 2))
    seq_block = pl.BlockSpec((None, t, w), lambda b, pt: (b, 0, 0))
    grid_spec = pltpu.PrefetchScalarGridSpec(
        num_scalar_prefetch=1,
        grid=(db,),
        in_specs=[seq_block, seq_block, seq_block,
                  pl.BlockSpec(memory_space=pl.ANY), pl.BlockSpec(memory_space=pl.ANY)],
        out_specs=seq_block,
        scratch_shapes=[pltpu.VMEM((2, MOBA_HEADS, MOBA_DIM, past), F32),
                        pltpu.VMEM((2, MOBA_HEADS, MOBA_DIM, past), F32),
                        pltpu.SemaphoreType.DMA((2,))])
    r3 = lambda a: a.reshape(db, t, w)
    return pl.pallas_call(
        functools.partial(_moba_sample_kernel, layer=layer, n_pages=n_pages),
        grid_spec=grid_spec,
        out_shape=jax.ShapeDtypeStruct((db, t, w), BF16),
        compiler_params=_cparams(("arbitrary",)),
        name="moba_sample",
    )(page_table, r3(mq_s), r3(mk_s), r3(mv_s), ck, cv).reshape(db * t, w)


def _merge_kernel(x_ref, a_ref, b_ref, c_ref, g_ref, wpa_ref, wpb_ref, wpc_ref, wout_ref, gffn_ref, wq_ref,
                  xo_ref, hn_ref, qp_ref):
    d = x_ref.shape[1]
    m = (g_ref[:, 0:d].astype(F32) * _dot(a_ref[...], wpa_ref[...])
         + g_ref[:, d:2 * d].astype(F32) * _dot(b_ref[...], wpb_ref[...])
         + g_ref[:, 2 * d:3 * d].astype(F32) * _dot(c_ref[...], wpc_ref[...]))
    x = x_ref[...] + _dot(m.astype(BF16), wout_ref[...])
    xo_ref[...] = x
    hn = _rms(x, gffn_ref[...]).astype(BF16)
    hn_ref[...] = hn
    qp_ref[...] = _dot(hn, wq_ref[...])


def _merge_call(x_all, a_all, b_all, c_all, gates, wpa_pad, wpb, wpc, wout, g_ffn, wq):
    t, d = x_all.shape
    tm = TOK_TILE
    row = lambda w: pl.BlockSpec((tm, w), lambda i: (i, 0))
    full = lambda a: pl.BlockSpec(a.shape, lambda i: (0,) * a.ndim)
    nq = wq.shape[1]
    return pl.pallas_call(
        _merge_kernel,
        grid=(t // tm,),
        in_specs=[row(d), row(a_all.shape[1]), row(b_all.shape[1]), row(c_all.shape[1]), row(3 * d),
                  full(wpa_pad), full(wpb), full(wpc), full(wout), full(g_ffn), full(wq)],
        out_specs=[row(d), row(d), row(nq)],
        out_shape=[jax.ShapeDtypeStruct((t, d), F32), jax.ShapeDtypeStruct((t, d), BF16),
                   jax.ShapeDtypeStruct((t, nq), F32)],
        compiler_params=_cparams(("parallel",)),
        name="merge",
    )(x_all, a_all, b_all, c_all, gates, wpa_pad, wpb, wpc, wout, g_ffn, wq)


def _top16(s):
    n = s.shape[0]
    rows = lax.broadcasted_iota(jnp.int32, s.shape, 0).astype(F32)
    vals = []
    rank = jnp.full(s.shape, float(PEER_TOPK), F32)
    for r in range(PEER_TOPK):
        m = jnp.max(s, axis=0, keepdims=True)
        idx = jnp.min(jnp.where(s == m, rows, float(n)), axis=0, keepdims=True)
        hit = rows == idx
        rank = jnp.where(hit, float(r), rank)
        s = jnp.where(hit, -jnp.inf, s)
        vals.append(m)
    return jnp.concatenate(vals, axis=0), rank


def _peer_select_kernel(q_ref, keys_ref, row_ref, map_ref):
    q = q_ref[...]
    half = q.shape[1] // 2
    s0 = _dot3_nt(keys_ref[0], q[:, :half])
    s1 = _dot3_nt(keys_ref[1], q[:, half:])
    sv0, rank0 = _top16(s0)
    sv1, rank1 = _top16(s1)
    k = PEER_TOPK
    ka = 4
    tl = q.shape[0]
    cand_a = (sv0[:ka, None, :] + sv1[None, :, :]).reshape(ka * k, tl)
    cand_b = (sv0[None, :, :] + sv1[:ka, None, :]).reshape(ka * k, tl)
    pos = lax.broadcasted_iota(jnp.int32, (ka * k, tl), 0)
    hi, lo = pos // k, pos % k
    ok_a = (hi + 1) * (lo + 1) <= k
    ok_b = ok_a & (lo >= ka)
    cand = jnp.concatenate([jnp.where(ok_a, cand_a, -jnp.inf), jnp.where(ok_b, cand_b, -jnp.inf)], axis=0)
    order = jnp.concatenate([jnp.where(ok_a, pos, 1000 + pos), jnp.where(ok_b, lo * k + hi, 2000 + pos)],
                            axis=0).astype(F32)
    picked = jnp.zeros(cand.shape, F32)
    z = jnp.zeros((1, tl), F32)
    top = None
    for r in range(k):
        m = jnp.max(cand, axis=0, keepdims=True)
        idx = jnp.min(jnp.where(cand == m, order, 1e9), axis=0, keepdims=True)
        hit = order == idx
        picked = jnp.where(hit, 1.0, picked)
        cand = jnp.where(hit, -jnp.inf, cand)
        if r == 0:
            top = m
        z = z + jnp.exp(m - top)
    cnt_a = jnp.sum(picked[:ka * k].reshape(ka, k, tl), axis=1)
    cnt_b = jnp.sum(picked[ka * k:].reshape(ka, k, tl), axis=0)
    count = jnp.zeros(s0.shape, F32)
    for r in range(k):
        cr = cnt_a[r:r + 1, :] if r < ka else cnt_b[r:r + 1, :]
        count = count + jnp.where(rank0 == float(r), cr, 0.0)
    row_ref[0] = count
    row_ref[1] = jnp.exp(s0 - sv0[0:1, :])
    map_ref[0] = rank1.astype(BF16)
    map_ref[1] = (jnp.exp(s1 - sv1[0:1, :]) / z).astype(BF16)


def _peer_select_call(qp, keys):
    t = qp.shape[0]
    heads, _, n_keys, half = keys.shape
    tl = SEL_TILE
    return pl.pallas_call(
        _peer_select_kernel,
        grid=(t // tl, heads),
        in_specs=[pl.BlockSpec((tl, 2 * half), lambda i, h: (i, h)),
                  pl.BlockSpec((None, 2, n_keys, half), lambda i, h: (h, 0, 0, 0))],
        out_specs=[pl.BlockSpec((None, 2, n_keys, tl), lambda i, h: (h, 0, 0, i)),
                   pl.BlockSpec((None, 2, n_keys, tl), lambda i, h: (h, 0, 0, i))],
        out_shape=[jax.ShapeDtypeStruct((heads, 2, n_keys, t), F32),
                   jax.ShapeDtypeStruct((heads, 2, n_keys, t), BF16)],
        compiler_params=_cparams(("parallel", "arbitrary")),
        name="peer_select",
    )(qp, keys)


def _peer_expert_kernel(hn_ref, x_ref, row_ref, map_ref, u_ref, vt_ref, gfin_ref, o_ref, acc_ref, *, final_norm):
    e = pl.program_id(1)
    n_keys = map_ref.shape[2]
    heads = map_ref.shape[0]
    tt = hn_ref.shape[0]

    @pl.when(e == 0)
    def _():
        acc_ref[...] = jnp.zeros_like(acc_ref)

    act = _gelu_tanh(_dot_nt(u_ref[...], hn_ref[...])).astype(BF16)
    sub = _BF16_SUBLANES
    shape3 = (n_keys // sub, sub, tt)
    gate = [jnp.zeros(shape3, BF16) for _ in range(PEER_ROWS)]
    for h in range(heads):
        rank1 = map_ref[h, 0].reshape(shape3)
        b = map_ref[h, 1].reshape(shape3)
        for ii in range(PEER_ROWS):
            i = e * PEER_ROWS + ii
            cnt = jnp.broadcast_to(row_ref[h, 0, pl.ds(i, 1), :], (sub, tt)).astype(BF16)
            a = jnp.broadcast_to(row_ref[h, 1, pl.ds(i, 1), :], (sub, tt)).astype(BF16)
            gate[ii] = gate[ii] + jnp.where(rank1 < cnt[None], a[None] * b, jnp.zeros_like(b))
    wt = jnp.concatenate([gate[ii].reshape(n_keys, tt) * act[ii * n_keys:(ii + 1) * n_keys]
                          for ii in range(PEER_ROWS)], axis=0)
    acc_ref[...] += _dot(vt_ref[...], wt)

    @pl.when(e == pl.num_programs(1) - 1)
    def _():
        x = x_ref[...] + acc_ref[...].T
        if final_norm:
            x = _rms(x, gfin_ref[...])
        o_ref[...] = x


def _peer_expert_call(hn, x_all, sel_rows, sel_maps, u_tab, v_tab_t, g_final, final_norm):
    t, d = x_all.shape
    heads, _, n_keys, _ = sel_maps.shape
    tt = PEER_TOK_TILE
    te = PEER_ROWS * n_keys
    return pl.pallas_call(
        functools.partial(_peer_expert_kernel, final_norm=final_norm),
        grid=(t // tt, n_keys // PEER_ROWS),
        in_specs=[pl.BlockSpec((tt, d), lambda i, e: (i, 0)),
                  pl.BlockSpec((tt, d), lambda i, e: (i, 0)),
                  pl.BlockSpec((heads, 2, n_keys, tt), lambda i, e: (0, 0, 0, i)),
                  pl.BlockSpec((heads, 2, n_keys, tt), lambda i, e: (0, 0, 0, i)),
                  pl.BlockSpec((te, d), lambda i, e: (e, 0)),
                  pl.BlockSpec((d, te), lambda i, e: (0, e)),
                  pl.BlockSpec((1, d), lambda i, e: (0, 0))],
        out_specs=pl.BlockSpec((tt, d), lambda i, e: (i, 0)),
        out_shape=jax.ShapeDtypeStruct((t, d), F32),
        scratch_shapes=[pltpu.VMEM((d, tt), F32)],
        compiler_params=_cparams(("parallel", "arbitrary")),
        name="peer_expert",
    )(hn, x_all, sel_rows, sel_maps, u_tab, v_tab_t, g_final)


def kernel(x_prompt, x_sample, cache_mla_latent, cache_mla_krope, cache_moba_k, cache_moba_v, page_table,
           g_mix, w_in, g_q_lat, w_uq, g_kv_lat, w_ukv, g_gm_v, b_gm_v, w_s, b_s,
           w_pa, w_pb, w_pc, w_out, g_ffn, w_peer_q, peer_keys, peer_u, peer_v, g_final):
    batch, seq, d = x_prompt.shape
    db, t_new, _ = x_sample.shape
    depth = w_in.shape[0]
    n_pages = page_table.shape[1]
    past = n_pages * PAGE_SIZE
    n_p, n_s = batch * seq, db * t_new
    assert n_p % TOK_TILE == 0 and n_s % TOK_TILE == 0 and GM_CHUNK % t_new == 0
    assert (n_p + n_s) % PEER_TOK_TILE == 0 and seq % ATT_TILE == 0

    pos = jnp.concatenate([jnp.tile(jnp.arange(seq, dtype=jnp.int32), batch),
                           jnp.tile(past + jnp.arange(t_new, dtype=jnp.int32), db)])
    tabs = _rope_tables(pos)
    x_all = jnp.concatenate([x_prompt.reshape(n_p, d), x_sample.reshape(n_s, d)], axis=0)
    row2 = lambda a: a.reshape(1, -1)
    kvl = w_ukv.shape[1]
    cache_pe_t = jnp.transpose(cache_mla_krope, (0, 1, 3, 2))

    outs = {k: [] for k in ("lat", "pe", "mk", "mv", "gv")}
    for l in range(depth):
        w_arr = _arrange_w_in(w_in[l])
        wuq, wuqr = _arrange_w_uq(w_uq[l])
        reps = GM_CHUNK // t_new
        w_small = w_s[l][:, :t_new, :t_new]
        eye = jnp.eye(reps, dtype=w_s.dtype)
        w_samp = jnp.einsum("ab,gts->gatbs", eye, w_small).reshape(GM_GROUPS, GM_CHUNK, GM_CHUNK)
        wmix = jnp.stack([w_s[l], w_samp])
        b_full = jnp.repeat(b_s[l].T, GM_DIM, axis=1)
        b_samp = jnp.tile(jnp.repeat(b_s[l][:, :t_new].T, GM_DIM, axis=1), (reps, 1))
        bmix = jnp.stack([b_full, b_samp])
        w3 = w_ukv[l].reshape(w_ukv.shape[1], MLA_HEADS, MLA_NOPE + MLA_V)
        wuk_pad = jnp.concatenate(
            [jnp.transpose(w3[..., :MLA_NOPE], (1, 2, 0)),
             jnp.zeros((MLA_HEADS, LANES - MLA_NOPE, w3.shape[0]), w3.dtype)], axis=1).astype(BF16)
        wuv_pad = jnp.concatenate(
            [jnp.zeros((MLA_HEADS, w3.shape[0], LANES - MLA_V), w3.dtype),
             jnp.transpose(w3[..., MLA_NOPE:], (1, 0, 2))], axis=2).astype(BF16)
        wpa3 = w_pa[l].reshape(MLA_HEADS, MLA_V, d)
        wpa_pad = jnp.concatenate([jnp.zeros((MLA_HEADS, LANES - MLA_V, d), w_pa.dtype), wpa3],
                                  axis=1).reshape(MLA_HEADS * LANES, d).astype(BF16)

        (ckv, kpe, mk, mv, gv, qall, kvx, kpep, mq, c_all, gates) = _proj_call(
            x_all, tabs, n_p // TOK_TILE, row2(g_mix[l]), w_arr, row2(g_q_lat[l]), wuq, wuqr,
            row2(g_kv_lat[l]), w_ukv[l].astype(BF16), row2(g_gm_v[l]), row2(b_gm_v[l]), wmix, bmix)

        a_p = _mla_prompt_call(qall, kvx, kpep, batch, seq)
        b_p = _moba_prompt_call(mq, mk, mv, batch, seq)
        a_s = _mla_sample_call(page_table, qall[n_p:], ckv[n_p:], kpep[n_p:], wuk_pad, wuv_pad,
                               cache_mla_latent, cache_pe_t, l)
        b_s_out = _moba_sample_call(page_table, mq[n_p:], mk[n_p:], mv[n_p:], cache_moba_k, cache_moba_v, l)
        a_all = jnp.concatenate([a_p, a_s], axis=0)
        b_all = jnp.concatenate([b_p, b_s_out], axis=0)

        x_mid, hn, qp = _merge_call(x_all, a_all, b_all, c_all, gates, wpa_pad, w_pb[l].astype(BF16),
                                    w_pc[l].astype(BF16), w_out[l].astype(BF16), row2(g_ffn[l]),
                                    w_peer_q[l].astype(BF16))
        sel_rows, sel_maps = _peer_select_call(qp, peer_keys[l])
        x_all = _peer_expert_call(hn, x_mid, sel_rows, sel_maps, peer_u[l].astype(BF16),
                                  peer_v[l].T.astype(BF16), row2(g_final), l == depth - 1)

        outs["lat"].append(ckv)
        outs["pe"].append(kpe)
        outs["mk"].append(mk)
        outs["mv"].append(mv)
        outs["gv"].append(gv)

    def split(name, tail):
        st = jnp.stack(outs[name])
        return (st[:, :n_p].reshape((depth, batch, seq) + tail), st[:, n_p:].reshape((depth, db, t_new) + tail))

    lat_p, lat_s = split("lat", (kvl,))
    pe_p, pe_s = split("pe", (MLA_ROPE,))
    mk_p, mk_s = split("mk", (MOBA_HEADS, MOBA_DIM))
    mv_p, mv_s = split("mv", (MOBA_HEADS, MOBA_DIM))
    _, gv_s = split("gv", (GM_GROUPS, GM_DIM))
    y_prompt = x_all[:n_p].reshape(batch, seq, d)
    y_sample = x_all[n_p:].reshape(db, t_new, d)
    return (y_prompt, y_sample, lat_p, pe_p, mk_p, mv_p, lat_s, pe_s, mk_s, mv_s, gv_s)
```

```python
import functools

import jax
import jax.numpy as jnp
import numpy as np
from jax import lax
from jax.experimental import pallas as pl
from jax.experimental.pallas import tpu as pltpu

F32 = jnp.float32
BF16 = jnp.bfloat16

MLA_HEADS = 8
MLA_NOPE = 64
MLA_ROPE = 32
MLA_V = 64
MOBA_HEADS = 4
MOBA_DIM = 64
MOBA_BLOCK = 256
MOBA_TOPK = 3
GM_GROUPS = 4
GM_DIM = 64
GM_CHUNK = 128
PEER_HEADS = 8
PEER_TOPK = 16
PAGE_SIZE = 128
ROPE_THETA = 10000.0
EPS = 1e-6
NEG_INF = -1e30

LANES = 128
_BF16_SUBLANES = 16
VMEM_LIMIT = 56 * 1024 * 1024

TOK_TILE = 256
ATT_TILE = 256
SEL_TILE = 256
PEER_TOK_TILE = 512
PEER_EXPERT_TILE = 512

_NT = (((1,), (1,)), ((), ()))


def _cparams(sem):
    return pltpu.CompilerParams(dimension_semantics=sem, vmem_limit_bytes=VMEM_LIMIT)


def _dot(a, b):
    return jnp.dot(a, b, preferred_element_type=F32)


def _dot_nt(a, b):
    return lax.dot_general(a, b, _NT, preferred_element_type=F32)


def _split(a):
    hi = a.astype(BF16)
    lo = (a - hi.astype(F32)).astype(BF16)
    return hi, lo


def _dot3_nt(a, b):
    ah, al = _split(a)
    bh, bl = _split(b)
    return _dot_nt(ah, bh) + _dot_nt(ah, bl) + _dot_nt(al, bh)


def _rms(x, g):
    return x * lax.rsqrt(jnp.mean(x * x, axis=-1, keepdims=True) + EPS) * g


def _gelu_tanh(x):
    c = float(np.sqrt(2.0 / np.pi))
    u = x * ((x * x) * (c * 0.044715) + c)
    hx = 0.5 * x
    return hx * jnp.tanh(u) + hx


def _lane_group(shape, width):
    return lax.broadcasted_iota(jnp.int32, shape, len(shape) - 1) // width


_C_QLAT, _C_KVLAT, _C_KPE, _C_KPER = 0, 256, 512, 640
_C_MQ, _C_MQR, _C_MK, _C_MKR, _C_MV, _C_U, _C_V, _C_G = 768, 1024, 1280, 1536, 1792, 2048, 2304, 2560


def _proj_kernel(x_ref, gmix_ref, w_ref, gq_ref, wuq_ref, wuqr_ref, gkv_ref,
                 ggv_ref, bgv_ref, wmix_ref, bmix_ref,
                 cosq_ref, sinq_ref, cosk_ref, sink_ref, cosm_ref, sinm_ref,
                 ckv_ref, kpe_ref, mk_ref, mv_ref, gv_ref,
                 qall_ref, kpep_ref, mq_ref, c_ref, gates_ref, ckvt_ref, mvt_ref):
    d_model = x_ref.shape[1]
    hb = _rms(x_ref[...], gmix_ref[...]).astype(BF16)

    def seg(off, width):
        return _dot(hb, w_ref[:, off:off + width])

    qn = _rms(seg(_C_QLAT, 256), gq_ref[...]).astype(BF16)
    cq, sq = cosq_ref[...], sinq_ref[...]
    for h in range(MLA_HEADS):
        sl = slice(h * 256, (h + 1) * 256)
        qa = _dot(qn, wuq_ref[:, sl])
        qr = _dot(qn, wuqr_ref[:, sl])
        qall_ref[:, sl] = (qa * cq + qr * sq).astype(BF16)

    ckv = _rms(seg(_C_KVLAT, 256), gkv_ref[...])
    ckv_ref[...] = ckv
    ckvt_ref[...] = ckv.T.astype(BF16)

    kpe = seg(_C_KPE, LANES) * cosk_ref[...] + seg(_C_KPER, LANES) * sink_ref[...]
    kpep_ref[...] = kpe.astype(BF16)
    kpe_ref[...] = kpe[:, :MLA_ROPE]

    cm, sm = cosm_ref[...], sinm_ref[...]
    mq_ref[...] = seg(_C_MQ, 256) * cm[:, :256] + seg(_C_MQR, 256) * sm[:, :256]
    mk_ref[...] = seg(_C_MK, 256) * cm[:, 256:] + seg(_C_MKR, 256) * sm[:, 256:]
    mv = seg(_C_MV, 256)
    mv_ref[...] = mv
    mvt_ref[...] = mv.T.astype(BF16)

    u = jax.nn.gelu(seg(_C_U, 256))
    gvx = jax.nn.gelu(seg(_C_V, 256))
    mu = jnp.mean(gvx, axis=-1, keepdims=True)
    xc = gvx - mu
    v = xc * lax.rsqrt(jnp.mean(xc * xc, axis=-1, keepdims=True) + EPS) * ggv_ref[...] + bgv_ref[...]
    gv_ref[...] = v
    tm = x_ref.shape[0]
    row = lax.broadcasted_iota(jnp.int32, (GM_CHUNK, GM_CHUNK), 0)
    col = lax.broadcasted_iota(jnp.int32, (GM_CHUNK, GM_CHUNK), 1)
    grp = _lane_group((GM_CHUNK, GM_GROUPS * GM_DIM), GM_DIM)
    for ci in range(tm // GM_CHUNK):
        rs = slice(ci * GM_CHUNK, (ci + 1) * GM_CHUNK)
        vc = v[rs]
        mixed = bmix_ref[...]
        for g in range(GM_GROUPS):
            wg = jnp.where(col <= row, wmix_ref[g], 0.0).astype(BF16)
            vg = jnp.where(grp == g, vc, 0.0).astype(BF16)
            mixed = mixed + _dot(wg, vg)
        c_ref[rs, :] = (u[rs] * mixed).astype(BF16)

    for j in range(3):
        gates_ref[:, j * d_model:(j + 1) * d_model] = jax.nn.sigmoid(
            seg(_C_G + j * d_model, d_model)).astype(BF16)


def _rot_cols(w, d):
    k, n = w.shape
    w3 = w.reshape(k, n // d, d)
    return jnp.concatenate([-w3[..., d // 2:], w3[..., :d // 2]], axis=-1).reshape(k, n)


def _rope_tables(pos):
    pos = pos.astype(F32)[:, None]

    def cs(d):
        inv = ROPE_THETA ** (-jnp.arange(0, d, 2, dtype=F32) / d)
        ang = pos * inv[None, :]
        return (jnp.concatenate([jnp.cos(ang)] * 2, axis=1), jnp.concatenate([jnp.sin(ang)] * 2, axis=1))

    n = pos.shape[0]
    c32, s32 = cs(MLA_ROPE)
    c64, s64 = cs(MOBA_DIM)
    ones, zeros = jnp.ones((n, LANES), F32), jnp.zeros((n, LANES), F32)
    pad = jnp.zeros((n, LANES - MLA_ROPE), F32)
    mla_scale = (MLA_NOPE + MLA_ROPE) ** -0.5
    cosq = jnp.concatenate([ones, c32, pad], axis=1) * mla_scale
    sinq = jnp.concatenate([zeros, s32, pad], axis=1) * mla_scale
    cosk = jnp.concatenate([c32, pad], axis=1)
    sink = jnp.concatenate([s32, pad], axis=1)
    moba_scale = MOBA_DIM ** -0.5
    c64h, s64h = jnp.tile(c64, (1, MOBA_HEADS)), jnp.tile(s64, (1, MOBA_HEADS))
    cosm = jnp.concatenate([c64h * moba_scale, c64h], axis=1)
    sinm = jnp.concatenate([s64h * moba_scale, s64h], axis=1)
    return cosq, sinq, cosk, sink, cosm, sinm


def _arrange_w_in(w_in):
    d = w_in.shape[0]
    o = np.cumsum([0, 256, 256, MLA_ROPE, 256, 256, 256, 256, 256])
    q_lat, kv_lat, k_rope, m_q, m_k, m_v, g_u, g_v = (w_in[:, o[i]:o[i + 1]] for i in range(8))
    gates = w_in[:, o[8]:]
    padk = jnp.zeros((d, LANES - MLA_ROPE), w_in.dtype)
    cols = [q_lat, kv_lat, k_rope, padk, _rot_cols(k_rope, MLA_ROPE), padk,
            m_q, _rot_cols(m_q, MOBA_DIM), m_k, _rot_cols(m_k, MOBA_DIM), m_v, g_u, g_v, gates]
    return jnp.concatenate(cols, axis=1).astype(BF16)


def _arrange_w_uq(w_uq):
    k = w_uq.shape[0]
    w3 = w_uq.reshape(k, MLA_HEADS, MLA_NOPE + MLA_ROPE)
    nope, pe = w3[..., :MLA_NOPE], w3[..., MLA_NOPE:]
    z64 = jnp.zeros((k, MLA_HEADS, LANES - MLA_NOPE), w_uq.dtype)
    z96 = jnp.zeros((k, MLA_HEADS, LANES - MLA_ROPE), w_uq.dtype)
    z128 = jnp.zeros((k, MLA_HEADS, LANES), w_uq.dtype)
    pe_rot = jnp.concatenate([-pe[..., MLA_ROPE // 2:], pe[..., :MLA_ROPE // 2]], axis=-1)
    big = jnp.concatenate([nope, z64, pe, z96], axis=-1).reshape(k, MLA_HEADS * 256)
    big_rot = jnp.concatenate([z128, pe_rot, z96], axis=-1).reshape(k, MLA_HEADS * 256)
    return big.astype(BF16), big_rot.astype(BF16)


def _proj_call(x_all, tabs, n_prompt_tiles, g_mix, w_arr, g_q, wuq, wuqr, g_kv, g_gv, b_gv, wmix, bmix):
    t, d = x_all.shape
    tm = TOK_TILE
    nt = t // tm
    row = lambda w: pl.BlockSpec((tm, w), lambda i: (i, 0))
    full = lambda a: pl.BlockSpec(a.shape, lambda i: (0,) * a.ndim)
    kind = lambda i: jnp.where(i >= n_prompt_tiles, 1, 0)
    in_specs = [row(d), full(g_mix), full(w_arr), full(g_q), full(wuq), full(wuqr), full(g_kv),
                full(g_gv), full(b_gv),
                pl.BlockSpec((None, GM_GROUPS, GM_CHUNK, GM_CHUNK), lambda i: (kind(i), 0, 0, 0)),
                pl.BlockSpec((None, GM_CHUNK, GM_GROUPS * GM_DIM), lambda i: (kind(i), 0, 0)),
                row(256), row(256), row(LANES), row(LANES), row(512), row(512)]
    outs = [((t, 256), F32), ((t, MLA_ROPE), F32), ((t, 256), F32), ((t, 256), F32), ((t, 256), F32),
            ((t, MLA_HEADS * 256), BF16), ((t, LANES), BF16),
            ((t, 256), F32), ((t, 256), BF16), ((t, 3 * d), BF16)]
    col = lambda h: pl.BlockSpec((None, h, tm), lambda i: (i, 0, 0))
    outs_t = [(nt, g_kv.shape[1], tm), (nt, MOBA_HEADS * MOBA_DIM, tm)]
    return pl.pallas_call(
        _proj_kernel,
        grid=(nt,),
        in_specs=in_specs,
        out_specs=[row(s[1]) for s, _ in outs] + [col(s[1]) for s in outs_t],
        out_shape=[jax.ShapeDtypeStruct(s, dt) for s, dt in outs] + [jax.ShapeDtypeStruct(s, BF16) for s in outs_t],
        compiler_params=_cparams(("parallel",)),
        name="proj",
    )(x_all, g_mix, w_arr, g_q, wuq, wuqr, g_kv, g_gv, b_gv, wmix, bmix, *tabs)


def _mla_prompt_kernel(q_ref, lat_ref, kpe_ref, latt_ref, wuk_ref, wuv_ref, o_ref, acc_ref):
    tq = q_ref.shape[0]
    qi = pl.program_id(1)
    q = q_ref[...]
    qs = []
    for h in range(MLA_HEADS):
        qa = _dot(q[:, h * 256:h * 256 + LANES], wuk_ref[h]).astype(BF16)
        qs.append(jnp.concatenate([qa, q[:, h * 256 + LANES:(h + 1) * 256]], axis=1))
    qs = jnp.concatenate(qs, axis=0)
    n = qs.shape[0]

    def tile(j, carry, masked):
        m, l = carry
        rows = pl.ds(pl.multiple_of(j * tq, tq), tq)
        keys = jnp.concatenate([lat_ref[rows, :].astype(BF16), kpe_ref[rows, :]], axis=1)
        s = _dot_nt(keys, qs)
        if masked:
            key = lax.broadcasted_iota(jnp.int32, s.shape, 0)
            qry = lax.broadcasted_iota(jnp.int32, s.shape, 1) % tq
            s = jnp.where(key <= qry, s, NEG_INF)
        m_new = jnp.maximum(m, jnp.max(s, axis=0, keepdims=True))
        alpha = jnp.exp(m - m_new)
        p = jnp.exp(s - m_new)
        l = alpha * l + jnp.sum(p, axis=0, keepdims=True)
        acc_ref[...] = alpha * acc_ref[...] + _dot(latt_ref[j], p.astype(BF16))
        return m_new, l

    acc_ref[...] = jnp.zeros_like(acc_ref)
    carry = tile(qi, (jnp.full((1, n), NEG_INF, F32), jnp.zeros((1, n), F32)), True)
    m, l = lax.fori_loop(0, qi, lambda j, c: tile(j, c, False), carry)
    o = acc_ref[...] / l
    for h in range(MLA_HEADS):
        o_h = o[:, h * tq:(h + 1) * tq].T.astype(BF16)
        o_ref[:, h * LANES:(h + 1) * LANES] = _dot(o_h, wuv_ref[h]).astype(BF16)


def _mla_prompt_call(qall, ckv, kpep, ckvt, wuk_pad, wuv_pad, batch, seq):
    tq = ATT_TILE
    nq = seq // tq
    kvl = ckv.shape[1]
    full = lambda a: pl.BlockSpec(a.shape, lambda b, i: (0,) * a.ndim)
    return pl.pallas_call(
        _mla_prompt_kernel,
        grid=(batch, nq),
        in_specs=[pl.BlockSpec((tq, MLA_HEADS * 256), lambda b, i: (b * nq + i, 0)),
                  pl.BlockSpec((seq, kvl), lambda b, i: (b, 0)),
                  pl.BlockSpec((seq, LANES), lambda b, i: (b, 0)),
                  pl.BlockSpec((nq, kvl, tq), lambda b, i: (b, 0, 0)),
                  full(wuk_pad), full(wuv_pad)],
        out_specs=pl.BlockSpec((tq, MLA_HEADS * LANES), lambda b, i: (b * nq + i, 0)),
        out_shape=jax.ShapeDtypeStruct((batch * seq, MLA_HEADS * LANES), BF16),
        scratch_shapes=[pltpu.VMEM((kvl, MLA_HEADS * tq), F32)],
        compiler_params=_cparams(("parallel", "arbitrary")),
        name="mla_prompt",
    )(qall, ckv, kpep, ckvt, wuk_pad, wuv_pad)


def _topk_bias(gate, n_valid, k):
    nblk = gate.shape[1]
    blk = lax.broadcasted_iota(jnp.int32, gate.shape, 1)
    valid = blk < n_valid
    g = jnp.where(valid, gate, NEG_INF)
    rank = jnp.zeros(gate.shape, F32)
    for n in range(nblk):
        gn = g[:, n:n + 1]
        ahead = (gn > g) | ((gn == g) & (n < blk))
        rank = rank + jnp.where(ahead, 1.0, 0.0)
    return jnp.where(valid & (rank < k), 0.0, NEG_INF)


def _stack_heads(q, heads, width):
    grp = _lane_group(q.shape, width)
    return jnp.concatenate([jnp.where(grp == h, q, jnp.zeros_like(q)) for h in range(heads)], axis=0)


def _unstack_heads(o, heads, width):
    rows = o.shape[0] // heads
    grp = _lane_group((rows, o.shape[1]), width)
    out = jnp.zeros((rows, o.shape[1]), F32)
    for h in range(heads):
        out = out + jnp.where(grp == h, o[h * rows:(h + 1) * rows], 0.0)
    return out


def _topk_bias_t(gate_t, n_valid, k):
    nblk = gate_t.shape[0]
    blk = lax.broadcasted_iota(jnp.int32, gate_t.shape, 0)
    valid = blk < n_valid
    g = jnp.where(valid, gate_t, NEG_INF)
    rank = jnp.zeros(gate_t.shape, F32)
    for n in range(nblk):
        gn = g[n:n + 1, :]
        ahead = (gn > g) | ((gn == g) & (n < blk))
        rank = rank + jnp.where(ahead, 1.0, 0.0)
    return jnp.where(valid & (rank < k), 0.0, NEG_INF)


def _moba_prompt_kernel(q_ref, k_ref, vt_ref, o_ref, kmean_ref, bias_ref, *, ksel):
    tq = q_ref.shape[0]
    nblk = k_ref.shape[0] // MOBA_BLOCK
    qi = pl.program_id(1)

    @pl.when(qi == 0)
    def _():
        for n in range(nblk):
            kmean_ref[n:n + 1, :] = jnp.mean(k_ref[n * MOBA_BLOCK:(n + 1) * MOBA_BLOCK, :], axis=0, keepdims=True)

    qf = _stack_heads(q_ref[...], MOBA_HEADS, MOBA_DIM)
    q = qf.astype(BF16)
    bias_ref[...] = _topk_bias_t(_dot3_nt(kmean_ref[...], qf), qi, ksel)

    def tile(j, carry, own):
        m, l, acc = carry
        rows = pl.ds(pl.multiple_of(j * MOBA_BLOCK, MOBA_BLOCK), MOBA_BLOCK)
        s = _dot_nt(k_ref[rows, :].astype(BF16), q)
        if own:
            key = lax.broadcasted_iota(jnp.int32, s.shape, 0)
            qry = lax.broadcasted_iota(jnp.int32, s.shape, 1) % tq
            s = jnp.where(key <= qry, s, NEG_INF)
        else:
            s = s + bias_ref[pl.ds(j, 1), :]
        m_new = jnp.maximum(m, jnp.max(s, axis=0, keepdims=True))
        alpha = jnp.exp(m - m_new)
        p = jnp.exp(s - m_new)
        l = alpha * l + jnp.sum(p, axis=0, keepdims=True)
        acc = alpha * acc + _dot(vt_ref[j], p.astype(BF16))
        return m_new, l, acc

    n = qf.shape[0]
    w = qf.shape[1]
    init = (jnp.full((1, n), NEG_INF, F32), jnp.zeros((1, n), F32), jnp.zeros((w, n), F32))
    carry = tile(qi, init, True)
    m, l, acc = lax.fori_loop(0, qi, lambda j, c: tile(j, c, False), carry)
    o = acc / l
    grp = lax.broadcasted_iota(jnp.int32, (w, tq), 0) // MOBA_DIM
    out_t = jnp.zeros((w, tq), F32)
    for h in range(MOBA_HEADS):
        out_t = out_t + jnp.where(grp == h, o[:, h * tq:(h + 1) * tq], 0.0)
    o_ref[...] = out_t.T.astype(BF16)


def _moba_prompt_call(mq, mk, mvt, batch, seq):
    assert ATT_TILE == MOBA_BLOCK == TOK_TILE and seq % MOBA_BLOCK == 0
    w = MOBA_HEADS * MOBA_DIM
    nq = seq // ATT_TILE
    ksel = min(MOBA_TOPK, (seq - 1) // MOBA_BLOCK)
    return pl.pallas_call(
        functools.partial(_moba_prompt_kernel, ksel=ksel),
        grid=(batch, nq),
        in_specs=[pl.BlockSpec((ATT_TILE, w), lambda b, i: (b * nq + i, 0)),
                  pl.BlockSpec((seq, w), lambda b, i: (b, 0)),
                  pl.BlockSpec((nq, w, ATT_TILE), lambda b, i: (b, 0, 0))],
        out_specs=pl.BlockSpec((ATT_TILE, w), lambda b, i: (b * nq + i, 0)),
        out_shape=jax.ShapeDtypeStruct((batch * seq, w), BF16),
        scratch_shapes=[pltpu.VMEM((nq, w), F32), pltpu.VMEM((nq, MOBA_HEADS * ATT_TILE), F32)],
        compiler_params=_cparams(("parallel", "arbitrary")),
        name="moba_prompt",
    )(mq, mk, mvt)


def _page_copies(pt_ref, seq_idx, layer, n_pages, slot, pairs, sem):
    copies = []
    for p in range(n_pages):
        page = pt_ref[seq_idx, p]
        rows = pl.ds(p * PAGE_SIZE, PAGE_SIZE)
        for cache, buf, rows_last in pairs:
            mid = (slice(None),) * (len(buf.shape) - 2)
            dst = buf.at[(slot,) + mid + (rows,)] if rows_last else buf.at[slot, rows, :]
            copies.append(pltpu.make_async_copy(cache.at[layer, page], dst, sem.at[slot]))
    return copies


def _gather_pages(pt_ref, layer, n_pages, pairs, sem):
    b = pl.program_id(0)
    nb = pl.num_programs(0)
    slot = b % 2

    @pl.when(b == 0)
    def _():
        for c in _page_copies(pt_ref, b, layer, n_pages, slot, pairs, sem):
            c.start()

    @pl.when(b + 1 < nb)
    def _():
        for c in _page_copies(pt_ref, b + 1, layer, n_pages, 1 - slot, pairs, sem):
            c.start()

    for c in _page_copies(pt_ref, b, layer, n_pages, slot, pairs, sem):
        c.wait()
    return slot


def _mla_sample_kernel(pt_ref, q_ref, ckv_ref, kpep_ref, wuk_ref, wuv_ref, lat_hbm, pe_hbm, o_ref,
                       lat_buf, pe_buf, sem, *, layer, n_pages):
    slot = _gather_pages(pt_ref, layer, n_pages, [(lat_hbm, lat_buf, False), (pe_hbm, pe_buf, True)], sem)
    t = q_ref.shape[0]
    q = q_ref[...]
    qa, qp = [], []
    for h in range(MLA_HEADS):
        qa.append(_dot(q[:, h * 256:h * 256 + LANES], wuk_ref[h]))
        qp.append(q[:, h * 256 + LANES:h * 256 + LANES + MLA_ROPE].astype(F32))
    qa = jnp.concatenate(qa, axis=0).astype(BF16)
    qp = jnp.concatenate(qp, axis=0).astype(BF16)
    lat = lat_buf[slot].astype(BF16)
    pe_t = pe_buf[slot].astype(BF16)
    s_p = _dot_nt(qa, lat) + _dot(qp, pe_t)
    ckv = ckv_ref[...].astype(BF16)
    kpn = kpep_ref[...][:, :MLA_ROPE]
    s_n = _dot_nt(qa, ckv) + _dot_nt(qp, kpn)
    r = lax.broadcasted_iota(jnp.int32, s_n.shape, 0) % t
    c = lax.broadcasted_iota(jnp.int32, s_n.shape, 1)
    s_n = jnp.where(c <= r, s_n, NEG_INF)
    m = jnp.maximum(jnp.max(s_p, axis=-1, keepdims=True), jnp.max(s_n, axis=-1, keepdims=True))
    p_p = jnp.exp(s_p - m)
    p_n = jnp.exp(s_n - m)
    l = jnp.sum(p_p, axis=-1, keepdims=True) + jnp.sum(p_n, axis=-1, keepdims=True)
    o_lat = (_dot(p_p.astype(BF16), lat) + _dot(p_n.astype(BF16), ckv)) / l
    for h in range(MLA_HEADS):
        o_ref[:, h * LANES:(h + 1) * LANES] = _dot(o_lat[h * t:(h + 1) * t].astype(BF16), wuv_ref[h]).astype(BF16)


def _mla_sample_call(page_table, qall_s, ckv_s, kpep_s, wuk_pad, wuv_pad, cache_lat, cache_pe, layer):
    db, n_pages = page_table.shape
    t = qall_s.shape[0] // db
    past = n_pages * PAGE_SIZE
    kvl = cache_lat.shape[-1]
    q3 = qall_s.reshape(db, t, MLA_HEADS * 256)
    c3 = ckv_s.reshape(db, t, kvl)
    k3 = kpep_s.reshape(db, t, LANES)
    seq_block = lambda w: pl.BlockSpec((None, t, w), lambda b, pt: (b, 0, 0))
    full = lambda a: pl.BlockSpec(a.shape, lambda b, pt: (0,) * a.ndim)
    grid_spec = pltpu.PrefetchScalarGridSpec(
        num_scalar_prefetch=1,
        grid=(db,),
        in_specs=[seq_block(MLA_HEADS * 256), seq_block(kvl), seq_block(LANES), full(wuk_pad), full(wuv_pad),
                  pl.BlockSpec(memory_space=pl.ANY), pl.BlockSpec(memory_space=pl.ANY)],
        out_specs=seq_block(MLA_HEADS * LANES),
        scratch_shapes=[pltpu.VMEM((2, past, kvl), F32), pltpu.VMEM((2, MLA_ROPE, past), F32),
                        pltpu.SemaphoreType.DMA((2,))])
    return pl.pallas_call(
        functools.partial(_mla_sample_kernel, layer=layer, n_pages=n_pages),
        grid_spec=grid_spec,
        out_shape=jax.ShapeDtypeStruct((db, t, MLA_HEADS * LANES), BF16),
        compiler_params=_cparams(("arbitrary",)),
        name="mla_sample",
    )(page_table, q3, c3, k3, wuk_pad, wuv_pad, cache_lat, cache_pe).reshape(db * t, MLA_HEADS * LANES)


def _moba_sample_kernel(pt_ref, q_ref, kn_ref, vn_ref, k_hbm, v_hbm, o_ref, k_buf, v_buf, sem,
                        *, layer, n_pages):
    slot = _gather_pages(pt_ref, layer, n_pages, [(k_hbm, k_buf, True), (v_hbm, v_buf, True)], sem)
    t = q_ref.shape[0]
    past = n_pages * PAGE_SIZE
    nblk = past // MOBA_BLOCK
    w = q_ref.shape[1]
    q = _stack_heads(q_ref[...], MOBA_HEADS, MOBA_DIM).astype(BF16)
    k_t = k_buf[slot].reshape(w, past).astype(BF16)
    s_raw = _dot(q, k_t)
    blk = lax.broadcasted_iota(jnp.int32, (q.shape[0], nblk), 1)
    gate = jnp.zeros((q.shape[0], nblk), F32)
    for n in range(nblk):
        gn = jnp.sum(s_raw[:, n * MOBA_BLOCK:(n + 1) * MOBA_BLOCK], axis=1, keepdims=True)
        gate = gate + jnp.where(blk == n, gn, 0.0)
    bias = _topk_bias(gate, nblk, min(MOBA_TOPK, nblk))
    expand = (lax.broadcasted_iota(jnp.int32, (nblk, past), 1) // MOBA_BLOCK
              == lax.broadcasted_iota(jnp.int32, (nblk, past), 0)).astype(BF16)
    s_p = s_raw + _dot(bias.astype(BF16), expand)
    kn = kn_ref[...].astype(BF16)
    s_n = _dot_nt(q, kn)
    r = lax.broadcasted_iota(jnp.int32, s_n.shape, 0) % t
    c = lax.broadcasted_iota(jnp.int32, s_n.shape, 1)
    s_n = jnp.where(c <= r, s_n, NEG_INF)
    m = jnp.maximum(jnp.max(s_p, axis=-1, keepdims=True), jnp.max(s_n, axis=-1, keepdims=True))
    p_p = jnp.exp(s_p - m)
    p_n = jnp.exp(s_n - m)
    l = jnp.sum(p_p, axis=-1, keepdims=True) + jnp.sum(p_n, axis=-1, keepdims=True)
    v_t = v_buf[slot].reshape(w, past).astype(BF16)
    o = (_dot_nt(p_p.astype(BF16), v_t) + _dot(p_n.astype(BF16), vn_ref[...].astype(BF16))) / l
    o_ref[...] = _unstack_heads(o, MOBA_HEADS, MOBA_DIM).astype(BF16)


def _moba_sample_call(page_table, mq_s, mk_s, mv_s, cache_k, cache_v, layer):
    db, n_pages = page_table.shape
    t = mq_s.shape[0] // db
    past = n_pages * PAGE_SIZE
    w = MOBA_HEADS * MOBA_DIM
    assert past % MOBA_BLOCK == 0 and past >= MOBA_BLOCK and t <= MOBA_BLOCK
    ck = jnp.transpose(cache_k, (0, 1, 3, 4, 2))
    cv = jnp.transpose(cache_v, (0, 1, 3, 4, 2))
    seq_block = pl.BlockSpec((None, t, w), lambda b, pt: (b, 0, 0))
    grid_spec = pltpu.PrefetchScalarGridSpec(
        num_scalar_prefetch=1,
        grid=(db,),
        in_specs=[seq_block, seq_block, seq_block,
                  pl.BlockSpec(memory_space=pl.ANY), pl.BlockSpec(memory_space=pl.ANY)],
        out_specs=seq_block,
        scratch_shapes=[pltpu.VMEM((2, MOBA_HEADS, MOBA_DIM, past), F32),
                        pltpu.VMEM((2, MOBA_HEADS, MOBA_DIM, past), F32),
                        pltpu.SemaphoreType.DMA((2,))])
    r3 = lambda a: a.reshape(db, t, w)
    return pl.pallas_call(
        functools.partial(_moba_sample_kernel, layer=layer, n_pages=n_pages),
        grid_spec=grid_spec,
        out_shape=jax.ShapeDtypeStruct((db, t, w), BF16),
        compiler_params=_cparams(("arbitrary",)),
        name="moba_sample",
    )(page_table, r3(mq_s), r3(mk_s), r3(mv_s), ck, cv).reshape(db * t, w)


def _merge_kernel(x_ref, a_ref, b_ref, c_ref, g_ref, wpa_ref, wpb_ref, wpc_ref, wout_ref, gffn_ref, wq_ref,
                  xo_ref, hn_ref, qp_ref):
    d = x_ref.shape[1]
    m = (g_ref[:, 0:d].astype(F32) * _dot(a_ref[...], wpa_ref[...])
         + g_ref[:, d:2 * d].astype(F32) * _dot(b_ref[...], wpb_ref[...])
         + g_ref[:, 2 * d:3 * d].astype(F32) * _dot(c_ref[...], wpc_ref[...]))
    x = x_ref[...] + _dot(m.astype(BF16), wout_ref[...])
    xo_ref[...] = x
    hn = _rms(x, gffn_ref[...]).astype(BF16)
    hn_ref[...] = hn
    wq_h = qp_ref.shape[2]
    for h in range(qp_ref.shape[0]):
        qp_ref[h] = _dot(hn, wq_ref[:, h * wq_h:(h + 1) * wq_h])


def _merge_call(x_all, a_all, b_all, c_all, gates, wpa_pad, wpb, wpc, wout, g_ffn, wq):
    t, d = x_all.shape
    tm = TOK_TILE
    row = lambda w: pl.BlockSpec((tm, w), lambda i: (i, 0))
    full = lambda a: pl.BlockSpec(a.shape, lambda i: (0,) * a.ndim)
    wq_h = wq.shape[1] // PEER_HEADS
    return pl.pallas_call(
        _merge_kernel,
        grid=(t // tm,),
        in_specs=[row(d), row(a_all.shape[1]), row(b_all.shape[1]), row(c_all.shape[1]), row(3 * d),
                  full(wpa_pad), full(wpb), full(wpc), full(wout), full(g_ffn), full(wq)],
        out_specs=[row(d), row(d), pl.BlockSpec((PEER_HEADS, tm, wq_h), lambda i: (0, i, 0))],
        out_shape=[jax.ShapeDtypeStruct((t, d), F32), jax.ShapeDtypeStruct((t, d), BF16),
                   jax.ShapeDtypeStruct((PEER_HEADS, t, wq_h), F32)],
        compiler_params=_cparams(("parallel",)),
        name="merge",
    )(x_all, a_all, b_all, c_all, gates, wpa_pad, wpb, wpc, wout, g_ffn, wq)


def _top16(s):
    n = s.shape[0]
    rows = lax.broadcasted_iota(jnp.int32, s.shape, 0).astype(F32)
    vals = []
    rank = jnp.full(s.shape, float(PEER_TOPK), F32)
    for r in range(PEER_TOPK):
        m = jnp.max(s, axis=0, keepdims=True)
        idx = jnp.min(jnp.where(s == m, rows, float(n)), axis=0, keepdims=True)
        hit = rows == idx
        rank = jnp.where(hit, float(r), rank)
        s = jnp.where(hit, -jnp.inf, s)
        vals.append(m)
    return jnp.concatenate(vals, axis=0), rank


def _select_head(q, keys0, keys1):
    half = q.shape[1] // 2
    s0 = _dot3_nt(keys0, q[:, :half])
    s1 = _dot3_nt(keys1, q[:, half:])
    sv0, rank0 = _top16(s0)
    sv1, rank1 = _top16(s1)
    k = PEER_TOPK
    ka = 4
    tl = q.shape[0]
    cand_a = (sv0[:ka, None, :] + sv1[None, :, :]).reshape(ka * k, tl)
    cand_b = (sv0[None, :, :] + sv1[:ka, None, :]).reshape(ka * k, tl)
    pos = lax.broadcasted_iota(jnp.int32, (ka * k, tl), 0)
    hi, lo = pos // k, pos % k
    ok_a = (hi + 1) * (lo + 1) <= k
    ok_b = ok_a & (lo >= ka)
    cand = jnp.concatenate([jnp.where(ok_a, cand_a, -jnp.inf), jnp.where(ok_b, cand_b, -jnp.inf)], axis=0)
    order = jnp.concatenate([jnp.where(ok_a, pos, 1000 + pos), jnp.where(ok_b, lo * k + hi, 2000 + pos)],
                            axis=0).astype(F32)
    picked = jnp.zeros(cand.shape, F32)
    z = jnp.zeros((1, tl), F32)
    top = None
    for r in range(k):
        m = jnp.max(cand, axis=0, keepdims=True)
        idx = jnp.min(jnp.where(cand == m, order, 1e9), axis=0, keepdims=True)
        hit = order == idx
        picked = jnp.where(hit, 1.0, picked)
        cand = jnp.where(hit, -jnp.inf, cand)
        if r == 0:
            top = m
        z = z + jnp.exp(m - top)
    cnt_a = jnp.sum(picked[:ka * k].reshape(ka, k, tl), axis=1)
    cnt_b = jnp.sum(picked[ka * k:].reshape(ka, k, tl), axis=0)
    count = jnp.zeros(s0.shape, F32)
    for r in range(k):
        cr = cnt_a[r:r + 1, :] if r < ka else cnt_b[r:r + 1, :]
        count = count + jnp.where(rank0 == float(r), cr, 0.0)
    return count, jnp.exp(s0 - sv0[0:1, :]), rank1.astype(BF16), (jnp.exp(s1 - sv1[0:1, :]) / z).astype(BF16)


def _peer_gate_kernel(q_ref, keys_ref, g_ref, row_ref, map_ref):
    heads, tl = q_ref.shape[0], q_ref.shape[1]
    n_keys = keys_ref.shape[2]

    def per_head(h, carry):
        count, a, rank1, b = _select_head(q_ref[h], keys_ref[h, 0], keys_ref[h, 1])
        row_ref[h, 0] = count
        row_ref[h, 1] = a
        map_ref[h, 0] = rank1
        map_ref[h, 1] = b
        return carry

    lax.fori_loop(0, heads, per_head, 0)

    sub = _BF16_SUBLANES
    shape3 = (n_keys // sub, sub, tl)

    def per_row(i, carry):
        gate = jnp.zeros(shape3, BF16)
        for h in range(heads):
            cnt = jnp.broadcast_to(row_ref[h, 0, pl.ds(i, 1), :], (sub, tl)).astype(BF16)
            a = jnp.broadcast_to(row_ref[h, 1, pl.ds(i, 1), :], (sub, tl)).astype(BF16)
            b = map_ref[h, 1].reshape(shape3)
            gate = gate + jnp.where(map_ref[h, 0].reshape(shape3) < cnt[None], a[None] * b, jnp.zeros_like(b))
        g_ref[pl.ds(pl.multiple_of(i * n_keys, n_keys), n_keys), :] = gate.reshape(n_keys, tl)
        return carry

    lax.fori_loop(0, n_keys, per_row, 0)


def _peer_gate_call(qp, keys):
    heads, t, wq_h = qp.shape
    _, _, n_keys, half = keys.shape
    tl = SEL_TILE
    return pl.pallas_call(
        _peer_gate_kernel,
        grid=(t // tl,),
        in_specs=[pl.BlockSpec((heads, tl, wq_h), lambda i: (0, i, 0)),
                  pl.BlockSpec(keys.shape, lambda i: (0, 0, 0, 0))],
        out_specs=pl.BlockSpec((n_keys * n_keys, tl), lambda i: (0, i)),
        out_shape=jax.ShapeDtypeStruct((n_keys * n_keys, t), BF16),
        scratch_shapes=[pltpu.VMEM((heads, 2, n_keys, tl), F32), pltpu.VMEM((heads, 2, n_keys, tl), BF16)],
        compiler_params=_cparams(("parallel",)),
        name="peer_gate",
    )(qp, keys)


def _peer_expert_kernel(hn_ref, x_ref, g_ref, u_ref, vt_ref, gfin_ref, o_ref, acc_ref, *, final_norm):
    e = pl.program_id(1)

    @pl.when(e == 0)
    def _():
        acc_ref[...] = jnp.zeros_like(acc_ref)

    act = _gelu_tanh(_dot_nt(u_ref[...], hn_ref[...])).astype(BF16)
    acc_ref[...] += _dot(vt_ref[...], g_ref[...] * act)

    @pl.when(e == pl.num_programs(1) - 1)
    def _():
        x = x_ref[...] + acc_ref[...].T
        if final_norm:
            x = _rms(x, gfin_ref[...])
        o_ref[...] = x


def _peer_expert_call(hn, x_all, gates, u_tab, v_tab_t, g_final, final_norm):
    t, d = x_all.shape
    tt = PEER_TOK_TILE
    te = PEER_EXPERT_TILE
    return pl.pallas_call(
        functools.partial(_peer_expert_kernel, final_norm=final_norm),
        grid=(t // tt, u_tab.shape[0] // te),
        in_specs=[pl.BlockSpec((tt, d), lambda i, e: (i, 0)),
                  pl.BlockSpec((tt, d), lambda i, e: (i, 0)),
                  pl.BlockSpec((te, tt), lambda i, e: (e, i)),
                  pl.BlockSpec((te, d), lambda i, e: (e, 0)),
                  pl.BlockSpec((d, te), lambda i, e: (0, e)),
                  pl.BlockSpec((1, d), lambda i, e: (0, 0))],
        out_specs=pl.BlockSpec((tt, d), lambda i, e: (i, 0)),
        out_shape=jax.ShapeDtypeStruct((t, d), F32),
        scratch_shapes=[pltpu.VMEM((d, tt), F32)],
        compiler_params=_cparams(("parallel", "arbitrary")),
        name="peer_expert",
    )(hn, x_all, gates, u_tab, v_tab_t, g_final)


def kernel(x_prompt, x_sample, cache_mla_latent, cache_mla_krope, cache_moba_k, cache_moba_v, page_table,
           g_mix, w_in, g_q_lat, w_uq, g_kv_lat, w_ukv, g_gm_v, b_gm_v, w_s, b_s,
           w_pa, w_pb, w_pc, w_out, g_ffn, w_peer_q, peer_keys, peer_u, peer_v, g_final):
    batch, seq, d = x_prompt.shape
    db, t_new, _ = x_sample.shape
    depth = w_in.shape[0]
    n_pages = page_table.shape[1]
    past = n_pages * PAGE_SIZE
    n_p, n_s = batch * seq, db * t_new
    assert n_p % TOK_TILE == 0 and n_s % TOK_TILE == 0 and GM_CHUNK % t_new == 0
    assert (n_p + n_s) % PEER_TOK_TILE == 0 and seq % ATT_TILE == 0

    pos = jnp.concatenate([jnp.tile(jnp.arange(seq, dtype=jnp.int32), batch),
                           jnp.tile(past + jnp.arange(t_new, dtype=jnp.int32), db)])
    tabs = _rope_tables(pos)
    x_all = jnp.concatenate([x_prompt.reshape(n_p, d), x_sample.reshape(n_s, d)], axis=0)
    row2 = lambda a: a.reshape(1, -1)
    kvl = w_ukv.shape[1]
    cache_pe_t = jnp.transpose(cache_mla_krope, (0, 1, 3, 2))

    outs = {k: [] for k in ("lat", "pe", "mk", "mv", "gv")}
    for l in range(depth):
        w_arr = _arrange_w_in(w_in[l])
        wuq, wuqr = _arrange_w_uq(w_uq[l])
        reps = GM_CHUNK // t_new
        w_small = w_s[l][:, :t_new, :t_new]
        eye = jnp.eye(reps, dtype=w_s.dtype)
        w_samp = jnp.einsum("ab,gts->gatbs", eye, w_small).reshape(GM_GROUPS, GM_CHUNK, GM_CHUNK)
        wmix = jnp.stack([w_s[l], w_samp])
        b_full = jnp.repeat(b_s[l].T, GM_DIM, axis=1)
        b_samp = jnp.tile(jnp.repeat(b_s[l][:, :t_new].T, GM_DIM, axis=1), (reps, 1))
        bmix = jnp.stack([b_full, b_samp])
        w3 = w_ukv[l].reshape(w_ukv.shape[1], MLA_HEADS, MLA_NOPE + MLA_V)
        wuk_pad = jnp.concatenate(
            [jnp.transpose(w3[..., :MLA_NOPE], (1, 2, 0)),
             jnp.zeros((MLA_HEADS, LANES - MLA_NOPE, w3.shape[0]), w3.dtype)], axis=1).astype(BF16)
        wuv_pad = jnp.concatenate(
            [jnp.zeros((MLA_HEADS, w3.shape[0], LANES - MLA_V), w3.dtype),
             jnp.transpose(w3[..., MLA_NOPE:], (1, 0, 2))], axis=2).astype(BF16)
        wpa3 = w_pa[l].reshape(MLA_HEADS, MLA_V, d)
        wpa_pad = jnp.concatenate([jnp.zeros((MLA_HEADS, LANES - MLA_V, d), w_pa.dtype), wpa3],
                                  axis=1).reshape(MLA_HEADS * LANES, d).astype(BF16)

        (ckv, kpe, mk, mv, gv, qall, kpep, mq, c_all, gates, ckvt, mvt) = _proj_call(
            x_all, tabs, n_p // TOK_TILE, row2(g_mix[l]), w_arr, row2(g_q_lat[l]), wuq, wuqr,
            row2(g_kv_lat[l]), row2(g_gm_v[l]), row2(b_gm_v[l]), wmix, bmix)

        a_p = _mla_prompt_call(qall, ckv, kpep, ckvt, wuk_pad, wuv_pad, batch, seq)
        b_p = _moba_prompt_call(mq, mk, mvt, batch, seq)
        a_s = _mla_sample_call(page_table, qall[n_p:], ckv[n_p:], kpep[n_p:], wuk_pad, wuv_pad,
                               cache_mla_latent, cache_pe_t, l)
        b_s_out = _moba_sample_call(page_table, mq[n_p:], mk[n_p:], mv[n_p:], cache_moba_k, cache_moba_v, l)
        a_all = jnp.concatenate([a_p, a_s], axis=0)
        b_all = jnp.concatenate([b_p, b_s_out], axis=0)

        x_mid, hn, qp = _merge_call(x_all, a_all, b_all, c_all, gates, wpa_pad, w_pb[l].astype(BF16),
                                    w_pc[l].astype(BF16), w_out[l].astype(BF16), row2(g_ffn[l]),
                                    w_peer_q[l].astype(BF16))
        peer_gates = _peer_gate_call(qp, peer_keys[l])
        x_all = _peer_expert_call(hn, x_mid, peer_gates, peer_u[l].astype(BF16),
                                  peer_v[l].T.astype(BF16), row2(g_final), l == depth - 1)

        outs["lat"].append(ckv)
        outs["pe"].append(kpe)
        outs["mk"].append(mk)
        outs["mv"].append(mv)
        outs["gv"].append(gv)

    def split(name, tail):
        st = jnp.stack(outs[name])
        return (st[:, :n_p].reshape((depth, batch, seq) + tail), st[:, n_p:].reshape((depth, db, t_new) + tail))

    lat_p, lat_s = split("lat", (kvl,))
    pe_p, pe_s = split("pe", (MLA_ROPE,))
    mk_p, mk_s = split("mk", (MOBA_HEADS, MOBA_DIM))
    mv_p, mv_s = split("mv", (MOBA_HEADS, MOBA_DIM))
    _, gv_s = split("gv", (GM_GROUPS, GM_DIM))
    y_prompt = x_all[:n_p].reshape(batch, seq, d)
    y_sample = x_all[n_p:].reshape(db, t_new, d)
    return (y_prompt, y_sample, lat_p, pe_p, mk_p, mv_p, lat_s, pe_s, mk_s, mv_s, gv_s)
```

```python
import functools

import jax
import jax.numpy as jnp
import numpy as np
from jax import lax
from jax.experimental import pallas as pl
from jax.experimental.pallas import tpu as pltpu

F32 = jnp.float32
BF16 = jnp.bfloat16

MLA_HEADS = 8
MLA_NOPE = 64
MLA_ROPE = 32
MLA_V = 64
MOBA_HEADS = 4
MOBA_DIM = 64
MOBA_BLOCK = 256
MOBA_TOPK = 3
GM_GROUPS = 4
GM_DIM = 64
GM_CHUNK = 128
PEER_HEADS = 8
PEER_TOPK = 16
PAGE_SIZE = 128
ROPE_THETA = 10000.0
EPS = 1e-6
NEG_INF = -1e30

LANES = 128
_BF16_SUBLANES = 16
VMEM_LIMIT = 56 * 1024 * 1024

TOK_TILE = 256
ATT_TILE = 256
SEL_TILE = 256
SEL_HEAD_GROUP = 4
PEER_TOK_TILE = 1024
PEER_EXPERT_TILE = 1024

_NT = (((1,), (1,)), ((), ()))


def _cparams(sem):
    return pltpu.CompilerParams(dimension_semantics=sem, vmem_limit_bytes=VMEM_LIMIT)


def _dot(a, b):
    return jnp.dot(a, b, preferred_element_type=F32)


def _dot_nt(a, b):
    return lax.dot_general(a, b, _NT, preferred_element_type=F32)


def _split(a):
    hi = a.astype(BF16)
    lo = (a - hi.astype(F32)).astype(BF16)
    return hi, lo


def _dot3_nt(a, b):
    ah, al = _split(a)
    bh, bl = _split(b)
    return _dot_nt(ah, bh) + _dot_nt(ah, bl) + _dot_nt(al, bh)


def _rms(x, g):
    return x * lax.rsqrt(jnp.mean(x * x, axis=-1, keepdims=True) + EPS) * g


def _gelu_tanh(x):
    c = float(np.sqrt(2.0 / np.pi))
    u = x * ((x * x) * (c * 0.044715) + c)
    hx = 0.5 * x
    return hx * jnp.tanh(u) + hx


def _lane_group(shape, width):
    return lax.broadcasted_iota(jnp.int32, shape, len(shape) - 1) // width


_C_QLAT, _C_KVLAT, _C_KPE, _C_KPER = 0, 256, 512, 640
_C_MQ, _C_MQR, _C_MK, _C_MKR, _C_MV, _C_U, _C_V, _C_G = 768, 1024, 1280, 1536, 1792, 2048, 2304, 2560


def _proj_kernel(x_ref, gmix_ref, w_ref, gq_ref, wuq_ref, wuqr_ref, gkv_ref,
                 ggv_ref, bgv_ref, wmix_ref, bmix_ref,
                 cosq_ref, sinq_ref, cosk_ref, sink_ref, cosm_ref, sinm_ref,
                 ckv_ref, kpe_ref, mk_ref, mv_ref, gv_ref,
                 qall_ref, kpep_ref, mq_ref, c_ref, gates_ref, ckvt_ref, mvt_ref):
    d_model = x_ref.shape[1]
    hb = _rms(x_ref[...], gmix_ref[...]).astype(BF16)

    def seg(off, width):
        return _dot(hb, w_ref[:, off:off + width])

    qn = _rms(seg(_C_QLAT, 256), gq_ref[...]).astype(BF16)
    cq, sq = cosq_ref[...], sinq_ref[...]
    for h in range(MLA_HEADS):
        sl = slice(h * 256, (h + 1) * 256)
        qa = _dot(qn, wuq_ref[:, sl])
        qr = _dot(qn, wuqr_ref[:, sl])
        qall_ref[:, sl] = (qa * cq + qr * sq).astype(BF16)

    ckv = _rms(seg(_C_KVLAT, 256), gkv_ref[...])
    ckv_ref[...] = ckv
    ckvt_ref[...] = ckv.T.astype(BF16)

    kpe = seg(_C_KPE, LANES) * cosk_ref[...] + seg(_C_KPER, LANES) * sink_ref[...]
    kpep_ref[...] = kpe.astype(BF16)
    kpe_ref[...] = kpe[:, :MLA_ROPE]

    cm, sm = cosm_ref[...], sinm_ref[...]
    mq_ref[...] = seg(_C_MQ, 256) * cm[:, :256] + seg(_C_MQR, 256) * sm[:, :256]
    mk_ref[...] = seg(_C_MK, 256) * cm[:, 256:] + seg(_C_MKR, 256) * sm[:, 256:]
    mv = seg(_C_MV, 256)
    mv_ref[...] = mv
    mvt_ref[...] = mv.T.astype(BF16)

    u = jax.nn.gelu(seg(_C_U, 256))
    gvx = jax.nn.gelu(seg(_C_V, 256))
    mu = jnp.mean(gvx, axis=-1, keepdims=True)
    xc = gvx - mu
    v = xc * lax.rsqrt(jnp.mean(xc * xc, axis=-1, keepdims=True) + EPS) * ggv_ref[...] + bgv_ref[...]
    gv_ref[...] = v
    tm = x_ref.shape[0]
    row = lax.broadcasted_iota(jnp.int32, (GM_CHUNK, GM_CHUNK), 0)
    col = lax.broadcasted_iota(jnp.int32, (GM_CHUNK, GM_CHUNK), 1)
    grp = _lane_group((GM_CHUNK, GM_GROUPS * GM_DIM), GM_DIM)
    for ci in range(tm // GM_CHUNK):
        rs = slice(ci * GM_CHUNK, (ci + 1) * GM_CHUNK)
        vc = v[rs]
        mixed = bmix_ref[...]
        for g in range(GM_GROUPS):
            wg = jnp.where(col <= row, wmix_ref[g], 0.0).astype(BF16)
            vg = jnp.where(grp == g, vc, 0.0).astype(BF16)
            mixed = mixed + _dot(wg, vg)
        c_ref[rs, :] = (u[rs] * mixed).astype(BF16)

    for j in range(3):
        gates_ref[:, j * d_model:(j + 1) * d_model] = jax.nn.sigmoid(
            seg(_C_G + j * d_model, d_model)).astype(BF16)


def _rot_cols(w, d):
    k, n = w.shape
    w3 = w.reshape(k, n // d, d)
    return jnp.concatenate([-w3[..., d // 2:], w3[..., :d // 2]], axis=-1).reshape(k, n)


def _rope_tables(pos):
    pos = pos.astype(F32)[:, None]

    def cs(d):
        inv = ROPE_THETA ** (-jnp.arange(0, d, 2, dtype=F32) / d)
        ang = pos * inv[None, :]
        return (jnp.concatenate([jnp.cos(ang)] * 2, axis=1), jnp.concatenate([jnp.sin(ang)] * 2, axis=1))

    n = pos.shape[0]
    c32, s32 = cs(MLA_ROPE)
    c64, s64 = cs(MOBA_DIM)
    ones, zeros = jnp.ones((n, LANES), F32), jnp.zeros((n, LANES), F32)
    pad = jnp.zeros((n, LANES - MLA_ROPE), F32)
    mla_scale = (MLA_NOPE + MLA_ROPE) ** -0.5
    cosq = jnp.concatenate([ones, c32, pad], axis=1) * mla_scale
    sinq = jnp.concatenate([zeros, s32, pad], axis=1) * mla_scale
    cosk = jnp.concatenate([c32, pad], axis=1)
    sink = jnp.concatenate([s32, pad], axis=1)
    moba_scale = MOBA_DIM ** -0.5
    c64h, s64h = jnp.tile(c64, (1, MOBA_HEADS)), jnp.tile(s64, (1, MOBA_HEADS))
    cosm = jnp.concatenate([c64h * moba_scale, c64h], axis=1)
    sinm = jnp.concatenate([s64h * moba_scale, s64h], axis=1)
    return cosq, sinq, cosk, sink, cosm, sinm


def _arrange_w_in(w_in):
    d = w_in.shape[0]
    o = np.cumsum([0, 256, 256, MLA_ROPE, 256, 256, 256, 256, 256])
    q_lat, kv_lat, k_rope, m_q, m_k, m_v, g_u, g_v = (w_in[:, o[i]:o[i + 1]] for i in range(8))
    gates = w_in[:, o[8]:]
    padk = jnp.zeros((d, LANES - MLA_ROPE), w_in.dtype)
    cols = [q_lat, kv_lat, k_rope, padk, _rot_cols(k_rope, MLA_ROPE), padk,
            m_q, _rot_cols(m_q, MOBA_DIM), m_k, _rot_cols(m_k, MOBA_DIM), m_v, g_u, g_v, gates]
    return jnp.concatenate(cols, axis=1).astype(BF16)


def _arrange_w_uq(w_uq):
    k = w_uq.shape[0]
    w3 = w_uq.reshape(k, MLA_HEADS, MLA_NOPE + MLA_ROPE)
    nope, pe = w3[..., :MLA_NOPE], w3[..., MLA_NOPE:]
    z64 = jnp.zeros((k, MLA_HEADS, LANES - MLA_NOPE), w_uq.dtype)
    z96 = jnp.zeros((k, MLA_HEADS, LANES - MLA_ROPE), w_uq.dtype)
    z128 = jnp.zeros((k, MLA_HEADS, LANES), w_uq.dtype)
    pe_rot = jnp.concatenate([-pe[..., MLA_ROPE // 2:], pe[..., :MLA_ROPE // 2]], axis=-1)
    big = jnp.concatenate([nope, z64, pe, z96], axis=-1).reshape(k, MLA_HEADS * 256)
    big_rot = jnp.concatenate([z128, pe_rot, z96], axis=-1).reshape(k, MLA_HEADS * 256)
    return big.astype(BF16), big_rot.astype(BF16)


def _proj_call(x_all, tabs, n_prompt_tiles, g_mix, w_arr, g_q, wuq, wuqr, g_kv, g_gv, b_gv, wmix, bmix):
    t, d = x_all.shape
    tm = TOK_TILE
    nt = t // tm
    row = lambda w: pl.BlockSpec((tm, w), lambda i: (i, 0))
    full = lambda a: pl.BlockSpec(a.shape, lambda i: (0,) * a.ndim)
    kind = lambda i: jnp.where(i >= n_prompt_tiles, 1, 0)
    in_specs = [row(d), full(g_mix), full(w_arr), full(g_q), full(wuq), full(wuqr), full(g_kv),
                full(g_gv), full(b_gv),
                pl.BlockSpec((None, GM_GROUPS, GM_CHUNK, GM_CHUNK), lambda i: (kind(i), 0, 0, 0)),
                pl.BlockSpec((None, GM_CHUNK, GM_GROUPS * GM_DIM), lambda i: (kind(i), 0, 0)),
                row(256), row(256), row(LANES), row(LANES), row(512), row(512)]
    outs = [((t, 256), F32), ((t, MLA_ROPE), F32), ((t, 256), F32), ((t, 256), F32), ((t, 256), F32),
            ((t, MLA_HEADS * 256), BF16), ((t, LANES), BF16),
            ((t, 256), F32), ((t, 256), BF16), ((t, 3 * d), BF16)]
    col = lambda h: pl.BlockSpec((None, h, tm), lambda i: (i, 0, 0))
    outs_t = [(nt, g_kv.shape[1], tm), (nt, MOBA_HEADS * MOBA_DIM, tm)]
    return pl.pallas_call(
        _proj_kernel,
        grid=(nt,),
        in_specs=in_specs,
        out_specs=[row(s[1]) for s, _ in outs] + [col(s[1]) for s in outs_t],
        out_shape=[jax.ShapeDtypeStruct(s, dt) for s, dt in outs] + [jax.ShapeDtypeStruct(s, BF16) for s in outs_t],
        compiler_params=_cparams(("parallel",)),
        name="proj",
    )(x_all, g_mix, w_arr, g_q, wuq, wuqr, g_kv, g_gv, b_gv, wmix, bmix, *tabs)


def _mla_prompt_kernel(q_ref, lat_ref, kpe_ref, latt_ref, wuk_ref, wuv_ref, o_ref, acc_ref):
    tq = q_ref.shape[0]
    qi = pl.program_id(1)
    q = q_ref[...]
    qs = []
    for h in range(MLA_HEADS):
        qa = _dot(q[:, h * 256:h * 256 + LANES], wuk_ref[h]).astype(BF16)
        qs.append(jnp.concatenate([qa, q[:, h * 256 + LANES:(h + 1) * 256]], axis=1))
    qs = jnp.concatenate(qs, axis=0)
    n = qs.shape[0]

    def tile(j, carry, masked):
        m, l = carry
        rows = pl.ds(pl.multiple_of(j * tq, tq), tq)
        keys = jnp.concatenate([lat_ref[rows, :].astype(BF16), kpe_ref[rows, :]], axis=1)
        s = _dot_nt(keys, qs)
        if masked:
            key = lax.broadcasted_iota(jnp.int32, s.shape, 0)
            qry = lax.broadcasted_iota(jnp.int32, s.shape, 1) % tq
            s = jnp.where(key <= qry, s, NEG_INF)
        m_new = jnp.maximum(m, jnp.max(s, axis=0, keepdims=True))
        alpha = jnp.exp(m - m_new)
        p = jnp.exp(s - m_new)
        l = alpha * l + jnp.sum(p, axis=0, keepdims=True)
        acc_ref[...] = alpha * acc_ref[...] + _dot(latt_ref[j], p.astype(BF16))
        return m_new, l

    acc_ref[...] = jnp.zeros_like(acc_ref)
    carry = tile(qi, (jnp.full((1, n), NEG_INF, F32), jnp.zeros((1, n), F32)), True)
    m, l = lax.fori_loop(0, qi, lambda j, c: tile(j, c, False), carry)
    o = acc_ref[...] / l
    for h in range(MLA_HEADS):
        o_h = o[:, h * tq:(h + 1) * tq].T.astype(BF16)
        o_ref[:, h * LANES:(h + 1) * LANES] = _dot(o_h, wuv_ref[h]).astype(BF16)


def _mla_prompt_call(qall, ckv, kpep, ckvt, wuk_pad, wuv_pad, batch, seq):
    tq = ATT_TILE
    nq = seq // tq
    kvl = ckv.shape[1]
    full = lambda a: pl.BlockSpec(a.shape, lambda b, i: (0,) * a.ndim)
    return pl.pallas_call(
        _mla_prompt_kernel,
        grid=(batch, nq),
        in_specs=[pl.BlockSpec((tq, MLA_HEADS * 256), lambda b, i: (b * nq + i, 0)),
                  pl.BlockSpec((seq, kvl), lambda b, i: (b, 0)),
                  pl.BlockSpec((seq, LANES), lambda b, i: (b, 0)),
                  pl.BlockSpec((nq, kvl, tq), lambda b, i: (b, 0, 0)),
                  full(wuk_pad), full(wuv_pad)],
        out_specs=pl.BlockSpec((tq, MLA_HEADS * LANES), lambda b, i: (b * nq + i, 0)),
        out_shape=jax.ShapeDtypeStruct((batch * seq, MLA_HEADS * LANES), BF16),
        scratch_shapes=[pltpu.VMEM((kvl, MLA_HEADS * tq), F32)],
        compiler_params=_cparams(("parallel", "arbitrary")),
        name="mla_prompt",
    )(qall, ckv, kpep, ckvt, wuk_pad, wuv_pad)


def _topk_bias(gate, n_valid, k):
    nblk = gate.shape[1]
    blk = lax.broadcasted_iota(jnp.int32, gate.shape, 1)
    valid = blk < n_valid
    g = jnp.where(valid, gate, NEG_INF)
    rank = jnp.zeros(gate.shape, F32)
    for n in range(nblk):
        gn = g[:, n:n + 1]
        ahead = (gn > g) | ((gn == g) & (n < blk))
        rank = rank + jnp.where(ahead, 1.0, 0.0)
    return jnp.where(valid & (rank < k), 0.0, NEG_INF)


def _stack_heads(q, heads, width):
    grp = _lane_group(q.shape, width)
    return jnp.concatenate([jnp.where(grp == h, q, jnp.zeros_like(q)) for h in range(heads)], axis=0)


def _unstack_heads(o, heads, width):
    rows = o.shape[0] // heads
    grp = _lane_group((rows, o.shape[1]), width)
    out = jnp.zeros((rows, o.shape[1]), F32)
    for h in range(heads):
        out = out + jnp.where(grp == h, o[h * rows:(h + 1) * rows], 0.0)
    return out


def _topk_bias_t(gate_t, n_valid, k):
    nblk = gate_t.shape[0]
    blk = lax.broadcasted_iota(jnp.int32, gate_t.shape, 0)
    valid = blk < n_valid
    g = jnp.where(valid, gate_t, NEG_INF)
    rank = jnp.zeros(gate_t.shape, F32)
    for n in range(nblk):
        gn = g[n:n + 1, :]
        ahead = (gn > g) | ((gn == g) & (n < blk))
        rank = rank + jnp.where(ahead, 1.0, 0.0)
    return jnp.where(valid & (rank < k), 0.0, NEG_INF)


def _moba_prompt_kernel(q_ref, k_ref, vt_ref, o_ref, kmean_ref, bias_ref, *, ksel):
    tq = q_ref.shape[0]
    nblk = k_ref.shape[0] // MOBA_BLOCK
    qi = pl.program_id(1)

    @pl.when(qi == 0)
    def _():
        for n in range(nblk):
            kmean_ref[n:n + 1, :] = jnp.mean(k_ref[n * MOBA_BLOCK:(n + 1) * MOBA_BLOCK, :], axis=0, keepdims=True)

    qf = _stack_heads(q_ref[...], MOBA_HEADS, MOBA_DIM)
    q = qf.astype(BF16)
    bias_ref[...] = _topk_bias_t(_dot3_nt(kmean_ref[...], qf), qi, ksel)

    def tile(j, carry, own):
        m, l, acc = carry
        rows = pl.ds(pl.multiple_of(j * MOBA_BLOCK, MOBA_BLOCK), MOBA_BLOCK)
        s = _dot_nt(k_ref[rows, :].astype(BF16), q)
        if own:
            key = lax.broadcasted_iota(jnp.int32, s.shape, 0)
            qry = lax.broadcasted_iota(jnp.int32, s.shape, 1) % tq
            s = jnp.where(key <= qry, s, NEG_INF)
        else:
            s = s + bias_ref[pl.ds(j, 1), :]
        m_new = jnp.maximum(m, jnp.max(s, axis=0, keepdims=True))
        alpha = jnp.exp(m - m_new)
        p = jnp.exp(s - m_new)
        l = alpha * l + jnp.sum(p, axis=0, keepdims=True)
        acc = alpha * acc + _dot(vt_ref[j], p.astype(BF16))
        return m_new, l, acc

    n = qf.shape[0]
    w = qf.shape[1]
    init = (jnp.full((1, n), NEG_INF, F32), jnp.zeros((1, n), F32), jnp.zeros((w, n), F32))
    carry = tile(qi, init, True)
    m, l, acc = lax.fori_loop(0, qi, lambda j, c: tile(j, c, False), carry)
    o = acc / l
    grp = lax.broadcasted_iota(jnp.int32, (w, tq), 0) // MOBA_DIM
    out_t = jnp.zeros((w, tq), F32)
    for h in range(MOBA_HEADS):
        out_t = out_t + jnp.where(grp == h, o[:, h * tq:(h + 1) * tq], 0.0)
    o_ref[...] = out_t.T.astype(BF16)


def _moba_prompt_call(mq, mk, mvt, batch, seq):
    assert ATT_TILE == MOBA_BLOCK == TOK_TILE and seq % MOBA_BLOCK == 0
    w = MOBA_HEADS * MOBA_DIM
    nq = seq // ATT_TILE
    ksel = min(MOBA_TOPK, (seq - 1) // MOBA_BLOCK)
    return pl.pallas_call(
        functools.partial(_moba_prompt_kernel, ksel=ksel),
        grid=(batch, nq),
        in_specs=[pl.BlockSpec((ATT_TILE, w), lambda b, i: (b * nq + i, 0)),
                  pl.BlockSpec((seq, w), lambda b, i: (b, 0)),
                  pl.BlockSpec((nq, w, ATT_TILE), lambda b, i: (b, 0, 0))],
        out_specs=pl.BlockSpec((ATT_TILE, w), lambda b, i: (b * nq + i, 0)),
        out_shape=jax.ShapeDtypeStruct((batch * seq, w), BF16),
        scratch_shapes=[pltpu.VMEM((nq, w), F32), pltpu.VMEM((nq, MOBA_HEADS * ATT_TILE), F32)],
        compiler_params=_cparams(("parallel", "arbitrary")),
        name="moba_prompt",
    )(mq, mk, mvt)


def _page_copies(pt_ref, seq_idx, layer, n_pages, slot, pairs, sem):
    copies = []
    for p in range(n_pages):
        page = pt_ref[seq_idx, p]
        rows = pl.ds(p * PAGE_SIZE, PAGE_SIZE)
        for cache, buf, rows_last in pairs:
            mid = (slice(None),) * (len(buf.shape) - 2)
            dst = buf.at[(slot,) + mid + (rows,)] if rows_last else buf.at[slot, rows, :]
            copies.append(pltpu.make_async_copy(cache.at[layer, page], dst, sem.at[slot]))
    return copies


def _gather_pages(pt_ref, layer, n_pages, pairs, sem):
    b = pl.program_id(0)
    nb = pl.num_programs(0)
    slot = b % 2

    @pl.when(b == 0)
    def _():
        for c in _page_copies(pt_ref, b, layer, n_pages, slot, pairs, sem):
            c.start()

    @pl.when(b + 1 < nb)
    def _():
        for c in _page_copies(pt_ref, b + 1, layer, n_pages, 1 - slot, pairs, sem):
            c.start()

    for c in _page_copies(pt_ref, b, layer, n_pages, slot, pairs, sem):
        c.wait()
    return slot


def _mla_sample_kernel(pt_ref, q_ref, ckv_ref, kpep_ref, wuk_ref, wuv_ref, lat_hbm, pe_hbm, o_ref,
                       lat_buf, pe_buf, sem, *, layer, n_pages):
    slot = _gather_pages(pt_ref, layer, n_pages, [(lat_hbm, lat_buf, False), (pe_hbm, pe_buf, True)], sem)
    t = q_ref.shape[0]
    q = q_ref[...]
    qa, qp = [], []
    for h in range(MLA_HEADS):
        qa.append(_dot(q[:, h * 256:h * 256 + LANES], wuk_ref[h]))
        qp.append(q[:, h * 256 + LANES:h * 256 + LANES + MLA_ROPE].astype(F32))
    qa = jnp.concatenate(qa, axis=0).astype(BF16)
    qp = jnp.concatenate(qp, axis=0).astype(BF16)
    lat = lat_buf[slot].astype(BF16)
    pe_t = pe_buf[slot].astype(BF16)
    s_p = _dot_nt(qa, lat) + _dot(qp, pe_t)
    ckv = ckv_ref[...].astype(BF16)
    kpn = kpep_ref[...][:, :MLA_ROPE]
    s_n = _dot_nt(qa, ckv) + _dot_nt(qp, kpn)
    r = lax.broadcasted_iota(jnp.int32, s_n.shape, 0) % t
    c = lax.broadcasted_iota(jnp.int32, s_n.shape, 1)
    s_n = jnp.where(c <= r, s_n, NEG_INF)
    m = jnp.maximum(jnp.max(s_p, axis=-1, keepdims=True), jnp.max(s_n, axis=-1, keepdims=True))
    p_p = jnp.exp(s_p - m)
    p_n = jnp.exp(s_n - m)
    l = jnp.sum(p_p, axis=-1, keepdims=True) + jnp.sum(p_n, axis=-1, keepdims=True)
    o_lat = (_dot(p_p.astype(BF16), lat) + _dot(p_n.astype(BF16), ckv)) / l
    for h in range(MLA_HEADS):
        o_ref[:, h * LANES:(h + 1) * LANES] = _dot(o_lat[h * t:(h + 1) * t].astype(BF16), wuv_ref[h]).astype(BF16)


def _mla_sample_call(page_table, qall_s, ckv_s, kpep_s, wuk_pad, wuv_pad, cache_lat, cache_pe, layer):
    db, n_pages = page_table.shape
    t = qall_s.shape[0] // db
    past = n_pages * PAGE_SIZE
    kvl = cache_lat.shape[-1]
    q3 = qall_s.reshape(db, t, MLA_HEADS * 256)
    c3 = ckv_s.reshape(db, t, kvl)
    k3 = kpep_s.reshape(db, t, LANES)
    seq_block = lambda w: pl.BlockSpec((None, t, w), lambda b, pt: (b, 0, 0))
    full = lambda a: pl.BlockSpec(a.shape, lambda b, pt: (0,) * a.ndim)
    grid_spec = pltpu.PrefetchScalarGridSpec(
        num_scalar_prefetch=1,
        grid=(db,),
        in_specs=[seq_block(MLA_HEADS * 256), seq_block(kvl), seq_block(LANES), full(wuk_pad), full(wuv_pad),
                  pl.BlockSpec(memory_space=pl.ANY), pl.BlockSpec(memory_space=pl.ANY)],
        out_specs=seq_block(MLA_HEADS * LANES),
        scratch_shapes=[pltpu.VMEM((2, past, kvl), F32), pltpu.VMEM((2, MLA_ROPE, past), F32),
                        pltpu.SemaphoreType.DMA((2,))])
    return pl.pallas_call(
        functools.partial(_mla_sample_kernel, layer=layer, n_pages=n_pages),
        grid_spec=grid_spec,
        out_shape=jax.ShapeDtypeStruct((db, t, MLA_HEADS * LANES), BF16),
        compiler_params=_cparams(("arbitrary",)),
        name="mla_sample",
    )(page_table, q3, c3, k3, wuk_pad, wuv_pad, cache_lat, cache_pe).reshape(db * t, MLA_HEADS * LANES)


def _moba_sample_kernel(pt_ref, q_ref, kn_ref, vn_ref, k_hbm, v_hbm, o_ref, k_buf, v_buf, sem,
                        *, layer, n_pages):
    slot = _gather_pages(pt_ref, layer, n_pages, [(k_hbm, k_buf, True), (v_hbm, v_buf, True)], sem)
    t = q_ref.shape[0]
    past = n_pages * PAGE_SIZE
    nblk = past // MOBA_BLOCK
    w = q_ref.shape[1]
    q = _stack_heads(q_ref[...], MOBA_HEADS, MOBA_DIM).astype(BF16)
    k_t = k_buf[slot].reshape(w, past).astype(BF16)
    s_raw = _dot(q, k_t)
    blk = lax.broadcasted_iota(jnp.int32, (q.shape[0], nblk), 1)
    gate = jnp.zeros((q.shape[0], nblk), F32)
    for n in range(nblk):
        gn = jnp.sum(s_raw[:, n * MOBA_BLOCK:(n + 1) * MOBA_BLOCK], axis=1, keepdims=True)
        gate = gate + jnp.where(blk == n, gn, 0.0)
    bias = _topk_bias(gate, nblk, min(MOBA_TOPK, nblk))
    expand = (lax.broadcasted_iota(jnp.int32, (nblk, past), 1) // MOBA_BLOCK
              == lax.broadcasted_iota(jnp.int32, (nblk, past), 0)).astype(BF16)
    s_p = s_raw + _dot(bias.astype(BF16), expand)
    kn = kn_ref[...].astype(BF16)
    s_n = _dot_nt(q, kn)
    r = lax.broadcasted_iota(jnp.int32, s_n.shape, 0) % t
    c = lax.broadcasted_iota(jnp.int32, s_n.shape, 1)
    s_n = jnp.where(c <= r, s_n, NEG_INF)
    m = jnp.maximum(jnp.max(s_p, axis=-1, keepdims=True), jnp.max(s_n, axis=-1, keepdims=True))
    p_p = jnp.exp(s_p - m)
    p_n = jnp.exp(s_n - m)
    l = jnp.sum(p_p, axis=-1, keepdims=True) + jnp.sum(p_n, axis=-1, keepdims=True)
    v_t = v_buf[slot].reshape(w, past).astype(BF16)
    o = (_dot_nt(p_p.astype(BF16), v_t) + _dot(p_n.astype(BF16), vn_ref[...].astype(BF16))) / l
    o_ref[...] = _unstack_heads(o, MOBA_HEADS, MOBA_DIM).astype(BF16)


def _moba_sample_call(page_table, mq_s, mk_s, mv_s, cache_k, cache_v, layer):
    db, n_pages = page_table.shape
    t = mq_s.shape[0] // db
    past = n_pages * PAGE_SIZE
    w = MOBA_HEADS * MOBA_DIM
    assert past % MOBA_BLOCK == 0 and past >= MOBA_BLOCK and t <= MOBA_BLOCK
    ck = jnp.transpose(cache_k, (0, 1, 3, 4, 2))
    cv = jnp.transpose(cache_v, (0, 1, 3, 4, 2))
    seq_block = pl.BlockSpec((None, t, w), lambda b, pt: (b, 0, 0))
    grid_spec = pltpu.PrefetchScalarGridSpec(
        num_scalar_prefetch=1,
        grid=(db,),
        in_specs=[seq_block, seq_block, seq_block,
                  pl.BlockSpec(memory_space=pl.ANY), pl.BlockSpec(memory_space=pl.ANY)],
        out_specs=seq_block,
        scratch_shapes=[pltpu.VMEM((2, MOBA_HEADS, MOBA_DIM, past), F32),
                        pltpu.VMEM((2, MOBA_HEADS, MOBA_DIM, past), F32),
                        pltpu.SemaphoreType.DMA((2,))])
    r3 = lambda a: a.reshape(db, t, w)
    return pl.pallas_call(
        functools.partial(_moba_sample_kernel, layer=layer, n_pages=n_pages),
        grid_spec=grid_spec,
        out_shape=jax.ShapeDtypeStruct((db, t, w), BF16),
        compiler_params=_cparams(("arbitrary",)),
        name="moba_sample",
    )(page_table, r3(mq_s), r3(mk_s), r3(mv_s), ck, cv).reshape(db * t, w)


def _merge_kernel(x_ref, a_ref, b_ref, c_ref, g_ref, wpa_ref, wpb_ref, wpc_ref, wout_ref, gffn_ref, wq_ref,
                  xo_ref, hnt_ref, qp_ref):
    d = x_ref.shape[1]
    m = (g_ref[:, 0:d].astype(F32) * _dot(a_ref[...], wpa_ref[...])
         + g_ref[:, d:2 * d].astype(F32) * _dot(b_ref[...], wpb_ref[...])
         + g_ref[:, 2 * d:3 * d].astype(F32) * _dot(c_ref[...], wpc_ref[...]))
    x = x_ref[...] + _dot(m.astype(BF16), wout_ref[...])
    xo_ref[...] = x
    hn32 = _rms(x, gffn_ref[...])
    hn = hn32.astype(BF16)
    hnt_ref[...] = hn32.T.astype(BF16)
    wq_h = qp_ref.shape[2]
    for h in range(qp_ref.shape[0]):
        qp_ref[h] = _dot(hn, wq_ref[:, h * wq_h:(h + 1) * wq_h])


def _merge_call(x_all, a_all, b_all, c_all, gates, wpa_pad, wpb, wpc, wout, g_ffn, wq):
    t, d = x_all.shape
    tm = TOK_TILE
    row = lambda w: pl.BlockSpec((tm, w), lambda i: (i, 0))
    full = lambda a: pl.BlockSpec(a.shape, lambda i: (0,) * a.ndim)
    wq_h = wq.shape[1] // PEER_HEADS
    return pl.pallas_call(
        _merge_kernel,
        grid=(t // tm,),
        in_specs=[row(d), row(a_all.shape[1]), row(b_all.shape[1]), row(c_all.shape[1]), row(3 * d),
                  full(wpa_pad), full(wpb), full(wpc), full(wout), full(g_ffn), full(wq)],
        out_specs=[row(d), pl.BlockSpec((d, tm), lambda i: (0, i)),
                   pl.BlockSpec((PEER_HEADS, tm, wq_h), lambda i: (0, i, 0))],
        out_shape=[jax.ShapeDtypeStruct((t, d), F32), jax.ShapeDtypeStruct((d, t), BF16),
                   jax.ShapeDtypeStruct((PEER_HEADS, t, wq_h), F32)],
        compiler_params=_cparams(("parallel",)),
        name="merge",
    )(x_all, a_all, b_all, c_all, gates, wpa_pad, wpb, wpc, wout, g_ffn, wq)


_TAKEN = 2.0 ** 126


def _top16(s):
    n = s.shape[0]
    rows = lax.broadcasted_iota(jnp.int32, s.shape, 0).astype(F32)
    vals = []
    for r in range(PEER_TOPK):
        m = jnp.max(s, axis=0, keepdims=True)
        idx = jnp.min(jnp.where(s == m, rows, float(n)), axis=0, keepdims=True)
        s = jnp.where(rows == idx, -_TAKEN * (1.0 + r / PEER_TOPK), s)
        vals.append(m)
    rank = jnp.where(s <= -_TAKEN, (s * (-1.0 / _TAKEN) - 1.0) * PEER_TOPK, float(PEER_TOPK))
    return jnp.concatenate(vals, axis=0), rank


def _select_head(q, keys0, keys1):
    half = q.shape[1] // 2
    s0 = _dot3_nt(keys0, q[:, :half])
    s1 = _dot3_nt(keys1, q[:, half:])
    sv0, rank0 = _top16(s0)
    sv1, rank1 = _top16(s1)
    k = PEER_TOPK
    ka = 4
    tl = q.shape[0]
    cand_a = (sv0[:ka, None, :] + sv1[None, :, :]).reshape(ka * k, tl)
    cand_b = (sv0[None, :, :] + sv1[:ka, None, :]).reshape(ka * k, tl)
    pos = lax.broadcasted_iota(jnp.int32, (ka * k, tl), 0)
    hi, lo = pos // k, pos % k
    ok_a = (hi + 1) * (lo + 1) <= k
    ok_b = ok_a & (lo >= ka)
    cand = jnp.concatenate([jnp.where(ok_a, cand_a, -jnp.inf), jnp.where(ok_b, cand_b, -jnp.inf)], axis=0)
    order = jnp.concatenate([jnp.where(ok_a, pos, 1000 + pos), jnp.where(ok_b, lo * k + hi, 2000 + pos)],
                            axis=0).astype(F32)
    z = jnp.zeros((1, tl), F32)
    top = None
    for r in range(k):
        m = jnp.max(cand, axis=0, keepdims=True)
        idx = jnp.min(jnp.where(cand == m, order, 1e9), axis=0, keepdims=True)
        cand = jnp.where(order == idx, -_TAKEN, cand)
        if r == 0:
            top = m
        z = z + jnp.exp(m - top)
    picked = jnp.where(cand == -_TAKEN, 1.0, 0.0)
    cnt_a = jnp.sum(picked[:ka * k].reshape(ka, k, tl), axis=1)
    cnt_b = jnp.sum(picked[ka * k:].reshape(ka, k, tl), axis=0)
    count = jnp.zeros(s0.shape, F32)
    for r in range(k):
        cr = cnt_a[r:r + 1, :] if r < ka else cnt_b[r:r + 1, :]
        count = count + jnp.where(rank0 == float(r), cr, 0.0)
    return count, jnp.exp(s0 - sv0[0:1, :]), rank1.astype(BF16), (jnp.exp(s1 - sv1[0:1, :]) / z).astype(BF16)


def _peer_gate_kernel(q_ref, keys_ref, g_ref, row_ref, map_ref):
    heads, tl = q_ref.shape[0], q_ref.shape[1]
    n_keys = keys_ref.shape[2]

    def per_head_group(hg, carry):
        for k in range(SEL_HEAD_GROUP):
            h = hg * SEL_HEAD_GROUP + k
            count, a, rank1, b = _select_head(q_ref[h], keys_ref[h, 0], keys_ref[h, 1])
            row_ref[h, 0] = count
            row_ref[h, 1] = a
            map_ref[h, 0] = rank1
            map_ref[h, 1] = b
        return carry

    lax.fori_loop(0, heads // SEL_HEAD_GROUP, per_head_group, 0)

    sub = _BF16_SUBLANES
    shape3 = (n_keys // sub, sub, tl)

    def per_row(i, carry):
        gate = jnp.zeros(shape3, BF16)
        for h in range(heads):
            cnt = jnp.broadcast_to(row_ref[h, 0, pl.ds(i, 1), :], (sub, tl)).astype(BF16)
            a = jnp.broadcast_to(row_ref[h, 1, pl.ds(i, 1), :], (sub, tl)).astype(BF16)
            b = map_ref[h, 1].reshape(shape3)
            gate = gate + jnp.where(map_ref[h, 0].reshape(shape3) < cnt[None], a[None] * b, jnp.zeros_like(b))
        g_ref[pl.ds(pl.multiple_of(i * n_keys, n_keys), n_keys), :] = gate.reshape(n_keys, tl)
        return carry

    lax.fori_loop(0, n_keys, per_row, 0)


def _peer_gate_call(qp, keys):
    heads, t, wq_h = qp.shape
    _, _, n_keys, half = keys.shape
    tl = SEL_TILE
    return pl.pallas_call(
        _peer_gate_kernel,
        grid=(t // tl,),
        in_specs=[pl.BlockSpec((heads, tl, wq_h), lambda i: (0, i, 0)),
                  pl.BlockSpec(keys.shape, lambda i: (0, 0, 0, 0))],
        out_specs=pl.BlockSpec((n_keys * n_keys, tl), lambda i: (0, i)),
        out_shape=jax.ShapeDtypeStruct((n_keys * n_keys, t), BF16),
        scratch_shapes=[pltpu.VMEM((heads, 2, n_keys, tl), F32), pltpu.VMEM((heads, 2, n_keys, tl), BF16)],
        compiler_params=_cparams(("parallel",)),
        name="peer_gate",
    )(qp, keys)


def _peer_expert_kernel(hnt_ref, x_ref, g_ref, u_ref, vt_ref, gfin_ref, o_ref, acc_ref, *, final_norm):
    e = pl.program_id(1)

    @pl.when(e == 0)
    def _():
        acc_ref[...] = jnp.zeros_like(acc_ref)

    act = _gelu_tanh(_dot(u_ref[...], hnt_ref[...])).astype(BF16)
    acc_ref[...] += _dot(vt_ref[...], g_ref[...] * act)

    @pl.when(e == pl.num_programs(1) - 1)
    def _():
        x = x_ref[...] + acc_ref[...].T
        if final_norm:
            x = _rms(x, gfin_ref[...])
        o_ref[...] = x


def _peer_expert_call(hnt, x_all, gates, u_tab, v_tab_t, g_final, final_norm):
    t, d = x_all.shape
    tt = PEER_TOK_TILE
    te = PEER_EXPERT_TILE
    return pl.pallas_call(
        functools.partial(_peer_expert_kernel, final_norm=final_norm),
        grid=(t // tt, u_tab.shape[0] // te),
        in_specs=[pl.BlockSpec((d, tt), lambda i, e: (0, i)),
                  pl.BlockSpec((tt, d), lambda i, e: (i, 0)),
                  pl.BlockSpec((te, tt), lambda i, e: (e, i)),
                  pl.BlockSpec((te, d), lambda i, e: (e, 0)),
                  pl.BlockSpec((d, te), lambda i, e: (0, e)),
                  pl.BlockSpec((1, d), lambda i, e: (0, 0))],
        out_specs=pl.BlockSpec((tt, d), lambda i, e: (i, 0)),
        out_shape=jax.ShapeDtypeStruct((t, d), F32),
        scratch_shapes=[pltpu.VMEM((d, tt), F32)],
        compiler_params=_cparams(("parallel", "arbitrary")),
        name="peer_expert",
    )(hnt, x_all, gates, u_tab, v_tab_t, g_final)


def kernel(x_prompt, x_sample, cache_mla_latent, cache_mla_krope, cache_moba_k, cache_moba_v, page_table,
           g_mix, w_in, g_q_lat, w_uq, g_kv_lat, w_ukv, g_gm_v, b_gm_v, w_s, b_s,
           w_pa, w_pb, w_pc, w_out, g_ffn, w_peer_q, peer_keys, peer_u, peer_v, g_final):
    batch, seq, d = x_prompt.shape
    db, t_new, _ = x_sample.shape
    depth = w_in.shape[0]
    n_pages = page_table.shape[1]
    past = n_pages * PAGE_SIZE
    n_p, n_s = batch * seq, db * t_new
    assert n_p % TOK_TILE == 0 and n_s % TOK_TILE == 0 and GM_CHUNK % t_new == 0
    assert (n_p + n_s) % PEER_TOK_TILE == 0 and seq % ATT_TILE == 0

    pos = jnp.concatenate([jnp.tile(jnp.arange(seq, dtype=jnp.int32), batch),
                           jnp.tile(past + jnp.arange(t_new, dtype=jnp.int32), db)])
    tabs = _rope_tables(pos)
    x_all = jnp.concatenate([x_prompt.reshape(n_p, d), x_sample.reshape(n_s, d)], axis=0)
    row2 = lambda a: a.reshape(1, -1)
    kvl = w_ukv.shape[1]
    cache_pe_t = jnp.transpose(cache_mla_krope, (0, 1, 3, 2))

    outs = {k: [] for k in ("lat", "pe", "mk", "mv", "gv")}
    for l in range(depth):
        w_arr = _arrange_w_in(w_in[l])
        wuq, wuqr = _arrange_w_uq(w_uq[l])
        reps = GM_CHUNK // t_new
        w_small = w_s[l][:, :t_new, :t_new]
        eye = jnp.eye(reps, dtype=w_s.dtype)
        w_samp = jnp.einsum("ab,gts->gatbs", eye, w_small).reshape(GM_GROUPS, GM_CHUNK, GM_CHUNK)
        wmix = jnp.stack([w_s[l], w_samp])
        b_full = jnp.repeat(b_s[l].T, GM_DIM, axis=1)
        b_samp = jnp.tile(jnp.repeat(b_s[l][:, :t_new].T, GM_DIM, axis=1), (reps, 1))
        bmix = jnp.stack([b_full, b_samp])
        w3 = w_ukv[l].reshape(w_ukv.shape[1], MLA_HEADS, MLA_NOPE + MLA_V)
        wuk_pad = jnp.concatenate(
            [jnp.transpose(w3[..., :MLA_NOPE], (1, 2, 0)),
             jnp.zeros((MLA_HEADS, LANES - MLA_NOPE, w3.shape[0]), w3.dtype)], axis=1).astype(BF16)
        wuv_pad = jnp.concatenate(
            [jnp.zeros((MLA_HEADS, w3.shape[0], LANES - MLA_V), w3.dtype),
             jnp.transpose(w3[..., MLA_NOPE:], (1, 0, 2))], axis=2).astype(BF16)
        wpa3 = w_pa[l].reshape(MLA_HEADS, MLA_V, d)
        wpa_pad = jnp.concatenate([jnp.zeros((MLA_HEADS, LANES - MLA_V, d), w_pa.dtype), wpa3],
                                  axis=1).reshape(MLA_HEADS * LANES, d).astype(BF16)

        (ckv, kpe, mk, mv, gv, qall, kpep, mq, c_all, gates, ckvt, mvt) = _proj_call(
            x_all, tabs, n_p // TOK_TILE, row2(g_mix[l]), w_arr, row2(g_q_lat[l]), wuq, wuqr,
            row2(g_kv_lat[l]), row2(g_gm_v[l]), row2(b_gm_v[l]), wmix, bmix)

        a_p = _mla_prompt_call(qall, ckv, kpep, ckvt, wuk_pad, wuv_pad, batch, seq)
        b_p = _moba_prompt_call(mq, mk, mvt, batch, seq)
        a_s = _mla_sample_call(page_table, qall[n_p:], ckv[n_p:], kpep[n_p:], wuk_pad, wuv_pad,
                               cache_mla_latent, cache_pe_t, l)
        b_s_out = _moba_sample_call(page_table, mq[n_p:], mk[n_p:], mv[n_p:], cache_moba_k, cache_moba_v, l)
        a_all = jnp.concatenate([a_p, a_s], axis=0)
        b_all = jnp.concatenate([b_p, b_s_out], axis=0)

        x_mid, hnt, qp = _merge_call(x_all, a_all, b_all, c_all, gates, wpa_pad, w_pb[l].astype(BF16),
                                     w_pc[l].astype(BF16), w_out[l].astype(BF16), row2(g_ffn[l]),
                                     w_peer_q[l].astype(BF16))
        peer_gates = _peer_gate_call(qp, peer_keys[l])
        x_all = _peer_expert_call(hnt, x_mid, peer_gates, peer_u[l].astype(BF16),
                                  peer_v[l].T.astype(BF16), row2(g_final), l == depth - 1)

        outs["lat"].append(ckv)
        outs["pe"].append(kpe)
        outs["mk"].append(mk)
        outs["mv"].append(mv)
        outs["gv"].append(gv)

    def split(name, tail):
        st = jnp.stack(outs[name])
        return (st[:, :n_p].reshape((depth, batch, seq) + tail), st[:, n_p:].reshape((depth, db, t_new) + tail))

    lat_p, lat_s = split("lat", (kvl,))
    pe_p, pe_s = split("pe", (MLA_ROPE,))
    mk_p, mk_s = split("mk", (MOBA_HEADS, MOBA_DIM))
    mv_p, mv_s = split("mv", (MOBA_HEADS, MOBA_DIM))
    _, gv_s = split("gv", (GM_GROUPS, GM_DIM))
    y_prompt = x_all[:n_p].reshape(batch, seq, d)
    y_sample = x_all[n_p:].reshape(db, t_new, d)
    return (y_prompt, y_sample, lat_p, pe_p, mk_p, mv_p, lat_s, pe_s, mk_s, mv_s, gv_s)
```

```python
import functools

import jax
import jax.numpy as jnp
import numpy as np
from jax import lax
from jax.experimental import pallas as pl
from jax.experimental.pallas import tpu as pltpu

F32 = jnp.float32
BF16 = jnp.bfloat16

MLA_HEADS = 8
MLA_NOPE = 64
MLA_ROPE = 32
MLA_V = 64
MOBA_HEADS = 4
MOBA_DIM = 64
MOBA_BLOCK = 256
MOBA_TOPK = 3
GM_GROUPS = 4
GM_DIM = 64
GM_CHUNK = 128
PEER_HEADS = 8
PEER_TOPK = 16
PAGE_SIZE = 128
ROPE_THETA = 10000.0
EPS = 1e-6
NEG_INF = -1e30

LANES = 128
_BF16_SUBLANES = 16
VMEM_LIMIT = 56 * 1024 * 1024

TOK_TILE = 256
ATT_TILE = 256
SEL_TILE = 256
SEL_HEAD_GROUP = 4
PEER_TOK_TILE = 1024
PEER_EXPERT_TILE = 1024

_NT = (((1,), (1,)), ((), ()))


def _cparams(sem):
    return pltpu.CompilerParams(dimension_semantics=sem, vmem_limit_bytes=VMEM_LIMIT)


def _dot(a, b):
    return jnp.dot(a, b, preferred_element_type=F32)


def _dot_nt(a, b):
    return lax.dot_general(a, b, _NT, preferred_element_type=F32)


def _split(a):
    hi = a.astype(BF16)
    lo = (a - hi.astype(F32)).astype(BF16)
    return hi, lo


def _dot3_nt(a, b):
    ah, al = _split(a)
    bh, bl = _split(b)
    return _dot_nt(ah, bh) + _dot_nt(ah, bl) + _dot_nt(al, bh)


def _rms(x, g):
    return x * lax.rsqrt(jnp.mean(x * x, axis=-1, keepdims=True) + EPS) * g


def _gelu_tanh(x):
    c = float(np.sqrt(2.0 / np.pi))
    u = x * ((x * x) * (c * 0.044715) + c)
    hx = 0.5 * x
    return hx * jnp.tanh(u) + hx


def _lane_group(shape, width):
    return lax.broadcasted_iota(jnp.int32, shape, len(shape) - 1) // width


_C_QLAT, _C_KVLAT, _C_KPE, _C_KPER = 0, 256, 512, 640
_C_MQ, _C_MQR, _C_MK, _C_MKR, _C_MV, _C_U, _C_V, _C_G = 768, 1024, 1280, 1536, 1792, 2048, 2304, 2560


def _proj_kernel(x_ref, gmix_ref, w_ref, gq_ref, wuq_ref, wuqr_ref, gkv_ref,
                 ggv_ref, bgv_ref, wmix_ref, bmix_ref,
                 cosq_ref, sinq_ref, cosk_ref, sink_ref, cosm_ref, sinm_ref,
                 ckv_ref, kpe_ref, mk_ref, mv_ref, gv_ref,
                 qall_ref, kpep_ref, mq_ref, c_ref, gates_ref, ckvt_ref, mvt_ref):
    d_model = x_ref.shape[1]
    hb = _rms(x_ref[...], gmix_ref[...]).astype(BF16)

    def seg(off, width):
        return _dot(hb, w_ref[:, off:off + width])

    qn = _rms(seg(_C_QLAT, 256), gq_ref[...]).astype(BF16)
    cq, sq = cosq_ref[...], sinq_ref[...]
    for h in range(MLA_HEADS):
        sl = slice(h * 256, (h + 1) * 256)
        qa = _dot(qn, wuq_ref[:, sl])
        qr = _dot(qn, wuqr_ref[:, sl])
        qall_ref[:, sl] = (qa * cq + qr * sq).astype(BF16)

    ckv = _rms(seg(_C_KVLAT, 256), gkv_ref[...])
    ckv_ref[...] = ckv
    ckvt_ref[...] = ckv.T.astype(BF16)

    kpe = seg(_C_KPE, LANES) * cosk_ref[...] + seg(_C_KPER, LANES) * sink_ref[...]
    kpep_ref[...] = kpe.astype(BF16)
    kpe_ref[...] = kpe[:, :MLA_ROPE]

    cm, sm = cosm_ref[...], sinm_ref[...]
    mq_ref[...] = seg(_C_MQ, 256) * cm[:, :256] + seg(_C_MQR, 256) * sm[:, :256]
    mk_ref[...] = seg(_C_MK, 256) * cm[:, 256:] + seg(_C_MKR, 256) * sm[:, 256:]
    mv = seg(_C_MV, 256)
    mv_ref[...] = mv
    mvt_ref[...] = mv.T.astype(BF16)

    u = jax.nn.gelu(seg(_C_U, 256))
    gvx = jax.nn.gelu(seg(_C_V, 256))
    mu = jnp.mean(gvx, axis=-1, keepdims=True)
    xc = gvx - mu
    v = xc * lax.rsqrt(jnp.mean(xc * xc, axis=-1, keepdims=True) + EPS) * ggv_ref[...] + bgv_ref[...]
    gv_ref[...] = v
    tm = x_ref.shape[0]
    row = lax.broadcasted_iota(jnp.int32, (GM_CHUNK, GM_CHUNK), 0)
    col = lax.broadcasted_iota(jnp.int32, (GM_CHUNK, GM_CHUNK), 1)
    grp = _lane_group((GM_CHUNK, GM_GROUPS * GM_DIM), GM_DIM)
    for ci in range(tm // GM_CHUNK):
        rs = slice(ci * GM_CHUNK, (ci + 1) * GM_CHUNK)
        vc = v[rs]
        mixed = bmix_ref[...]
        for g in range(GM_GROUPS):
            wg = jnp.where(col <= row, wmix_ref[g], 0.0).astype(BF16)
            vg = jnp.where(grp == g, vc, 0.0).astype(BF16)
            mixed = mixed + _dot(wg, vg)
        c_ref[rs, :] = (u[rs] * mixed).astype(BF16)

    for j in range(3):
        gates_ref[:, j * d_model:(j + 1) * d_model] = jax.nn.sigmoid(
            seg(_C_G + j * d_model, d_model)).astype(BF16)


def _rot_cols(w, d):
    k, n = w.shape
    w3 = w.reshape(k, n // d, d)
    return jnp.concatenate([-w3[..., d // 2:], w3[..., :d // 2]], axis=-1).reshape(k, n)


def _rope_tables(pos):
    pos = pos.astype(F32)[:, None]

    def cs(d):
        inv = ROPE_THETA ** (-jnp.arange(0, d, 2, dtype=F32) / d)
        ang = pos * inv[None, :]
        return (jnp.concatenate([jnp.cos(ang)] * 2, axis=1), jnp.concatenate([jnp.sin(ang)] * 2, axis=1))

    n = pos.shape[0]
    c32, s32 = cs(MLA_ROPE)
    c64, s64 = cs(MOBA_DIM)
    ones, zeros = jnp.ones((n, LANES), F32), jnp.zeros((n, LANES), F32)
    pad = jnp.zeros((n, LANES - MLA_ROPE), F32)
    mla_scale = (MLA_NOPE + MLA_ROPE) ** -0.5
    cosq = jnp.concatenate([ones, c32, pad], axis=1) * mla_scale
    sinq = jnp.concatenate([zeros, s32, pad], axis=1) * mla_scale
    cosk = jnp.concatenate([c32, pad], axis=1)
    sink = jnp.concatenate([s32, pad], axis=1)
    moba_scale = MOBA_DIM ** -0.5
    c64h, s64h = jnp.tile(c64, (1, MOBA_HEADS)), jnp.tile(s64, (1, MOBA_HEADS))
    cosm = jnp.concatenate([c64h * moba_scale, c64h], axis=1)
    sinm = jnp.concatenate([s64h * moba_scale, s64h], axis=1)
    return cosq, sinq, cosk, sink, cosm, sinm


def _arrange_w_in(w_in):
    d = w_in.shape[0]
    o = np.cumsum([0, 256, 256, MLA_ROPE, 256, 256, 256, 256, 256])
    q_lat, kv_lat, k_rope, m_q, m_k, m_v, g_u, g_v = (w_in[:, o[i]:o[i + 1]] for i in range(8))
    gates = w_in[:, o[8]:]
    padk = jnp.zeros((d, LANES - MLA_ROPE), w_in.dtype)
    cols = [q_lat, kv_lat, k_rope, padk, _rot_cols(k_rope, MLA_ROPE), padk,
            m_q, _rot_cols(m_q, MOBA_DIM), m_k, _rot_cols(m_k, MOBA_DIM), m_v, g_u, g_v, gates]
    return jnp.concatenate(cols, axis=1).astype(BF16)


def _arrange_w_uq(w_uq):
    k = w_uq.shape[0]
    w3 = w_uq.reshape(k, MLA_HEADS, MLA_NOPE + MLA_ROPE)
    nope, pe = w3[..., :MLA_NOPE], w3[..., MLA_NOPE:]
    z64 = jnp.zeros((k, MLA_HEADS, LANES - MLA_NOPE), w_uq.dtype)
    z96 = jnp.zeros((k, MLA_HEADS, LANES - MLA_ROPE), w_uq.dtype)
    z128 = jnp.zeros((k, MLA_HEADS, LANES), w_uq.dtype)
    pe_rot = jnp.concatenate([-pe[..., MLA_ROPE // 2:], pe[..., :MLA_ROPE // 2]], axis=-1)
    big = jnp.concatenate([nope, z64, pe, z96], axis=-1).reshape(k, MLA_HEADS * 256)
    big_rot = jnp.concatenate([z128, pe_rot, z96], axis=-1).reshape(k, MLA_HEADS * 256)
    return big.astype(BF16), big_rot.astype(BF16)


def _proj_call(x_all, tabs, n_prompt_tiles, g_mix, w_arr, g_q, wuq, wuqr, g_kv, g_gv, b_gv, wmix, bmix):
    t, d = x_all.shape
    tm = TOK_TILE
    nt = t // tm
    row = lambda w: pl.BlockSpec((tm, w), lambda i: (i, 0))
    full = lambda a: pl.BlockSpec(a.shape, lambda i: (0,) * a.ndim)
    kind = lambda i: jnp.where(i >= n_prompt_tiles, 1, 0)
    in_specs = [row(d), full(g_mix), full(w_arr), full(g_q), full(wuq), full(wuqr), full(g_kv),
                full(g_gv), full(b_gv),
                pl.BlockSpec((None, GM_GROUPS, GM_CHUNK, GM_CHUNK), lambda i: (kind(i), 0, 0, 0)),
                pl.BlockSpec((None, GM_CHUNK, GM_GROUPS * GM_DIM), lambda i: (kind(i), 0, 0)),
                row(256), row(256), row(LANES), row(LANES), row(512), row(512)]
    outs = [((t, 256), F32), ((t, MLA_ROPE), F32), ((t, 256), F32), ((t, 256), F32), ((t, 256), F32),
            ((t, MLA_HEADS * 256), BF16), ((t, LANES), BF16),
            ((t, 256), F32), ((t, 256), BF16), ((t, 3 * d), BF16)]
    col = lambda h: pl.BlockSpec((None, h, tm), lambda i: (i, 0, 0))
    outs_t = [(nt, g_kv.shape[1], tm), (nt, MOBA_HEADS * MOBA_DIM, tm)]
    return pl.pallas_call(
        _proj_kernel,
        grid=(nt,),
        in_specs=in_specs,
        out_specs=[row(s[1]) for s, _ in outs] + [col(s[1]) for s in outs_t],
        out_shape=[jax.ShapeDtypeStruct(s, dt) for s, dt in outs] + [jax.ShapeDtypeStruct(s, BF16) for s in outs_t],
        compiler_params=_cparams(("parallel",)),
        name="proj",
    )(x_all, g_mix, w_arr, g_q, wuq, wuqr, g_kv, g_gv, b_gv, wmix, bmix, *tabs)


def _mla_prompt_kernel(q_ref, lat_ref, kpe_ref, latt_ref, wuk_ref, wuv_ref, o_ref, acc_ref):
    tq = q_ref.shape[0]
    qi = pl.program_id(1)
    q = q_ref[...]
    qs = []
    for h in range(MLA_HEADS):
        qa = _dot(q[:, h * 256:h * 256 + LANES], wuk_ref[h]).astype(BF16)
        qs.append(jnp.concatenate([qa, q[:, h * 256 + LANES:(h + 1) * 256]], axis=1))
    qs = jnp.concatenate(qs, axis=0)
    n = qs.shape[0]

    def tile(j, carry, masked):
        m, l = carry
        rows = pl.ds(pl.multiple_of(j * tq, tq), tq)
        keys = jnp.concatenate([lat_ref[rows, :].astype(BF16), kpe_ref[rows, :]], axis=1)
        s = _dot_nt(keys, qs)
        if masked:
            key = lax.broadcasted_iota(jnp.int32, s.shape, 0)
            qry = lax.broadcasted_iota(jnp.int32, s.shape, 1) % tq
            s = jnp.where(key <= qry, s, NEG_INF)
        m_new = jnp.maximum(m, jnp.max(s, axis=0, keepdims=True))
        alpha = jnp.exp(m - m_new)
        p = jnp.exp(s - m_new)
        l = alpha * l + jnp.sum(p, axis=0, keepdims=True)
        acc_ref[...] = alpha * acc_ref[...] + _dot(latt_ref[j], p.astype(BF16))
        return m_new, l

    acc_ref[...] = jnp.zeros_like(acc_ref)
    carry = tile(qi, (jnp.full((1, n), NEG_INF, F32), jnp.zeros((1, n), F32)), True)
    m, l = lax.fori_loop(0, qi, lambda j, c: tile(j, c, False), carry)
    o = acc_ref[...] / l
    for h in range(MLA_HEADS):
        o_h = o[:, h * tq:(h + 1) * tq].T.astype(BF16)
        o_ref[:, h * LANES:(h + 1) * LANES] = _dot(o_h, wuv_ref[h]).astype(BF16)


def _mla_prompt_call(qall, ckv, kpep, ckvt, wuk_pad, wuv_pad, batch, seq):
    tq = ATT_TILE
    nq = seq // tq
    kvl = ckv.shape[1]
    full = lambda a: pl.BlockSpec(a.shape, lambda b, i: (0,) * a.ndim)
    return pl.pallas_call(
        _mla_prompt_kernel,
        grid=(batch, nq),
        in_specs=[pl.BlockSpec((tq, MLA_HEADS * 256), lambda b, i: (b * nq + i, 0)),
                  pl.BlockSpec((seq, kvl), lambda b, i: (b, 0)),
                  pl.BlockSpec((seq, LANES), lambda b, i: (b, 0)),
                  pl.BlockSpec((nq, kvl, tq), lambda b, i: (b, 0, 0)),
                  full(wuk_pad), full(wuv_pad)],
        out_specs=pl.BlockSpec((tq, MLA_HEADS * LANES), lambda b, i: (b * nq + i, 0)),
        out_shape=jax.ShapeDtypeStruct((batch * seq, MLA_HEADS * LANES), BF16),
        scratch_shapes=[pltpu.VMEM((kvl, MLA_HEADS * tq), F32)],
        compiler_params=_cparams(("parallel", "arbitrary")),
        name="mla_prompt",
    )(qall, ckv, kpep, ckvt, wuk_pad, wuv_pad)


def _topk_bias(gate, n_valid, k):
    nblk = gate.shape[1]
    blk = lax.broadcasted_iota(jnp.int32, gate.shape, 1)
    valid = blk < n_valid
    g = jnp.where(valid, gate, NEG_INF)
    rank = jnp.zeros(gate.shape, F32)
    for n in range(nblk):
        gn = g[:, n:n + 1]
        ahead = (gn > g) | ((gn == g) & (n < blk))
        rank = rank + jnp.where(ahead, 1.0, 0.0)
    return jnp.where(valid & (rank < k), 0.0, NEG_INF)


def _stack_heads(q, heads, width):
    grp = _lane_group(q.shape, width)
    return jnp.concatenate([jnp.where(grp == h, q, jnp.zeros_like(q)) for h in range(heads)], axis=0)


def _unstack_heads(o, heads, width):
    rows = o.shape[0] // heads
    grp = _lane_group((rows, o.shape[1]), width)
    out = jnp.zeros((rows, o.shape[1]), F32)
    for h in range(heads):
        out = out + jnp.where(grp == h, o[h * rows:(h + 1) * rows], 0.0)
    return out


def _topk_bias_t(gate_t, n_valid, k):
    nblk = gate_t.shape[0]
    blk = lax.broadcasted_iota(jnp.int32, gate_t.shape, 0)
    valid = blk < n_valid
    g = jnp.where(valid, gate_t, NEG_INF)
    rank = jnp.zeros(gate_t.shape, F32)
    for n in range(nblk):
        gn = g[n:n + 1, :]
        ahead = (gn > g) | ((gn == g) & (n < blk))
        rank = rank + jnp.where(ahead, 1.0, 0.0)
    return jnp.where(valid & (rank < k), 0.0, NEG_INF)


def _moba_prompt_kernel(q_ref, k_ref, vt_ref, o_ref, kmean_ref, bias_ref, *, ksel):
    tq = q_ref.shape[0]
    nblk = k_ref.shape[0] // MOBA_BLOCK
    qi = pl.program_id(1)

    @pl.when(qi == 0)
    def _():
        for n in range(nblk):
            kmean_ref[n:n + 1, :] = jnp.mean(k_ref[n * MOBA_BLOCK:(n + 1) * MOBA_BLOCK, :], axis=0, keepdims=True)

    qf = _stack_heads(q_ref[...], MOBA_HEADS, MOBA_DIM)
    q = qf.astype(BF16)
    bias_ref[...] = _topk_bias_t(_dot3_nt(kmean_ref[...], qf), qi, ksel)

    def tile(j, carry, own):
        m, l, acc = carry
        rows = pl.ds(pl.multiple_of(j * MOBA_BLOCK, MOBA_BLOCK), MOBA_BLOCK)
        s = _dot_nt(k_ref[rows, :].astype(BF16), q)
        if own:
            key = lax.broadcasted_iota(jnp.int32, s.shape, 0)
            qry = lax.broadcasted_iota(jnp.int32, s.shape, 1) % tq
            s = jnp.where(key <= qry, s, NEG_INF)
        else:
            s = s + bias_ref[pl.ds(j, 1), :]
        m_new = jnp.maximum(m, jnp.max(s, axis=0, keepdims=True))
        alpha = jnp.exp(m - m_new)
        p = jnp.exp(s - m_new)
        l = alpha * l + jnp.sum(p, axis=0, keepdims=True)
        acc = alpha * acc + _dot(vt_ref[j], p.astype(BF16))
        return m_new, l, acc

    n = qf.shape[0]
    w = qf.shape[1]
    init = (jnp.full((1, n), NEG_INF, F32), jnp.zeros((1, n), F32), jnp.zeros((w, n), F32))
    carry = tile(qi, init, True)
    m, l, acc = lax.fori_loop(0, qi, lambda j, c: tile(j, c, False), carry)
    o = acc / l
    grp = lax.broadcasted_iota(jnp.int32, (w, tq), 0) // MOBA_DIM
    out_t = jnp.zeros((w, tq), F32)
    for h in range(MOBA_HEADS):
        out_t = out_t + jnp.where(grp == h, o[:, h * tq:(h + 1) * tq], 0.0)
    o_ref[...] = out_t.T.astype(BF16)


def _moba_prompt_call(mq, mk, mvt, batch, seq):
    assert ATT_TILE == MOBA_BLOCK == TOK_TILE and seq % MOBA_BLOCK == 0
    w = MOBA_HEADS * MOBA_DIM
    nq = seq // ATT_TILE
    ksel = min(MOBA_TOPK, (seq - 1) // MOBA_BLOCK)
    return pl.pallas_call(
        functools.partial(_moba_prompt_kernel, ksel=ksel),
        grid=(batch, nq),
        in_specs=[pl.BlockSpec((ATT_TILE, w), lambda b, i: (b * nq + i, 0)),
                  pl.BlockSpec((seq, w), lambda b, i: (b, 0)),
                  pl.BlockSpec((nq, w, ATT_TILE), lambda b, i: (b, 0, 0))],
        out_specs=pl.BlockSpec((ATT_TILE, w), lambda b, i: (b * nq + i, 0)),
        out_shape=jax.ShapeDtypeStruct((batch * seq, w), BF16),
        scratch_shapes=[pltpu.VMEM((nq, w), F32), pltpu.VMEM((nq, MOBA_HEADS * ATT_TILE), F32)],
        compiler_params=_cparams(("parallel", "arbitrary")),
        name="moba_prompt",
    )(mq, mk, mvt)


def _page_copies(pt_ref, seq_idx, layer, n_pages, slot, pairs, sem):
    copies = []
    for p in range(n_pages):
        page = pt_ref[seq_idx, p]
        rows = pl.ds(p * PAGE_SIZE, PAGE_SIZE)
        for cache, buf, rows_last in pairs:
            mid = (slice(None),) * (len(buf.shape) - 2)
            dst = buf.at[(slot,) + mid + (rows,)] if rows_last else buf.at[slot, rows, :]
            copies.append(pltpu.make_async_copy(cache.at[layer, page], dst, sem.at[slot]))
    return copies


def _gather_pages(pt_ref, layer, n_pages, pairs, sem):
    b = pl.program_id(0)
    nb = pl.num_programs(0)
    slot = b % 2

    @pl.when(b == 0)
    def _():
        for c in _page_copies(pt_ref, b, layer, n_pages, slot, pairs, sem):
            c.start()

    @pl.when(b + 1 < nb)
    def _():
        for c in _page_copies(pt_ref, b + 1, layer, n_pages, 1 - slot, pairs, sem):
            c.start()

    for c in _page_copies(pt_ref, b, layer, n_pages, slot, pairs, sem):
        c.wait()
    return slot


def _mla_sample_kernel(pt_ref, q_ref, ckv_ref, kpep_ref, wuk_ref, wuv_ref, lat_hbm, pe_hbm, o_ref,
                       lat_buf, pe_buf, sem, *, layer, n_pages):
    slot = _gather_pages(pt_ref, layer, n_pages, [(lat_hbm, lat_buf, False), (pe_hbm, pe_buf, True)], sem)
    t = q_ref.shape[0]
    q = q_ref[...]
    qa, qp = [], []
    for h in range(MLA_HEADS):
        qa.append(_dot(q[:, h * 256:h * 256 + LANES], wuk_ref[h]))
        qp.append(q[:, h * 256 + LANES:h * 256 + LANES + MLA_ROPE].astype(F32))
    qa = jnp.concatenate(qa, axis=0).astype(BF16)
    qp = jnp.concatenate(qp, axis=0).astype(BF16)
    lat = lat_buf[slot].astype(BF16)
    pe_t = pe_buf[slot].astype(BF16)
    s_p = _dot_nt(qa, lat) + _dot(qp, pe_t)
    ckv = ckv_ref[...].astype(BF16)
    kpn = kpep_ref[...][:, :MLA_ROPE]
    s_n = _dot_nt(qa, ckv) + _dot_nt(qp, kpn)
    r = lax.broadcasted_iota(jnp.int32, s_n.shape, 0) % t
    c = lax.broadcasted_iota(jnp.int32, s_n.shape, 1)
    s_n = jnp.where(c <= r, s_n, NEG_INF)
    m = jnp.maximum(jnp.max(s_p, axis=-1, keepdims=True), jnp.max(s_n, axis=-1, keepdims=True))
    p_p = jnp.exp(s_p - m)
    p_n = jnp.exp(s_n - m)
    l = jnp.sum(p_p, axis=-1, keepdims=True) + jnp.sum(p_n, axis=-1, keepdims=True)
    o_lat = (_dot(p_p.astype(BF16), lat) + _dot(p_n.astype(BF16), ckv)) / l
    for h in range(MLA_HEADS):
        o_ref[:, h * LANES:(h + 1) * LANES] = _dot(o_lat[h * t:(h + 1) * t].astype(BF16), wuv_ref[h]).astype(BF16)


def _mla_sample_call(page_table, qall_s, ckv_s, kpep_s, wuk_pad, wuv_pad, cache_lat, cache_pe, layer):
    db, n_pages = page_table.shape
    t = qall_s.shape[0] // db
    past = n_pages * PAGE_SIZE
    kvl = cache_lat.shape[-1]
    q3 = qall_s.reshape(db, t, MLA_HEADS * 256)
    c3 = ckv_s.reshape(db, t, kvl)
    k3 = kpep_s.reshape(db, t, LANES)
    seq_block = lambda w: pl.BlockSpec((None, t, w), lambda b, pt: (b, 0, 0))
    full = lambda a: pl.BlockSpec(a.shape, lambda b, pt: (0,) * a.ndim)
    grid_spec = pltpu.PrefetchScalarGridSpec(
        num_scalar_prefetch=1,
        grid=(db,),
        in_specs=[seq_block(MLA_HEADS * 256), seq_block(kvl), seq_block(LANES), full(wuk_pad), full(wuv_pad),
                  pl.BlockSpec(memory_space=pl.ANY), pl.BlockSpec(memory_space=pl.ANY)],
        out_specs=seq_block(MLA_HEADS * LANES),
        scratch_shapes=[pltpu.VMEM((2, past, kvl), F32), pltpu.VMEM((2, MLA_ROPE, past), F32),
                        pltpu.SemaphoreType.DMA((2,))])
    return pl.pallas_call(
        functools.partial(_mla_sample_kernel, layer=layer, n_pages=n_pages),
        grid_spec=grid_spec,
        out_shape=jax.ShapeDtypeStruct((db, t, MLA_HEADS * LANES), BF16),
        compiler_params=_cparams(("arbitrary",)),
        name="mla_sample",
    )(page_table, q3, c3, k3, wuk_pad, wuv_pad, cache_lat, cache_pe).reshape(db * t, MLA_HEADS * LANES)


def _moba_sample_kernel(pt_ref, q_ref, kn_ref, vn_ref, k_hbm, v_hbm, o_ref, k_buf, v_buf, sem,
                        *, layer, n_pages):
    slot = _gather_pages(pt_ref, layer, n_pages, [(k_hbm, k_buf, True), (v_hbm, v_buf, True)], sem)
    t = q_ref.shape[0]
    past = n_pages * PAGE_SIZE
    nblk = past // MOBA_BLOCK
    w = q_ref.shape[1]
    q = _stack_heads(q_ref[...], MOBA_HEADS, MOBA_DIM).astype(BF16)
    k_t = k_buf[slot].reshape(w, past).astype(BF16)
    s_raw = _dot(q, k_t)
    blk = lax.broadcasted_iota(jnp.int32, (q.shape[0], nblk), 1)
    gate = jnp.zeros((q.shape[0], nblk), F32)
    for n in range(nblk):
        gn = jnp.sum(s_raw[:, n * MOBA_BLOCK:(n + 1) * MOBA_BLOCK], axis=1, keepdims=True)
        gate = gate + jnp.where(blk == n, gn, 0.0)
    bias = _topk_bias(gate, nblk, min(MOBA_TOPK, nblk))
    expand = (lax.broadcasted_iota(jnp.int32, (nblk, past), 1) // MOBA_BLOCK
              == lax.broadcasted_iota(jnp.int32, (nblk, past), 0)).astype(BF16)
    s_p = s_raw + _dot(bias.astype(BF16), expand)
    kn = kn_ref[...].astype(BF16)
    s_n = _dot_nt(q, kn)
    r = lax.broadcasted_iota(jnp.int32, s_n.shape, 0) % t
    c = lax.broadcasted_iota(jnp.int32, s_n.shape, 1)
    s_n = jnp.where(c <= r, s_n, NEG_INF)
    m = jnp.maximum(jnp.max(s_p, axis=-1, keepdims=True), jnp.max(s_n, axis=-1, keepdims=True))
    p_p = jnp.exp(s_p - m)
    p_n = jnp.exp(s_n - m)
    l = jnp.sum(p_p, axis=-1, keepdims=True) + jnp.sum(p_n, axis=-1, keepdims=True)
    v_t = v_buf[slot].reshape(w, past).astype(BF16)
    o = (_dot_nt(p_p.astype(BF16), v_t) + _dot(p_n.astype(BF16), vn_ref[...].astype(BF16))) / l
    o_ref[...] = _unstack_heads(o, MOBA_HEADS, MOBA_DIM).astype(BF16)


def _moba_sample_call(page_table, mq_s, mk_s, mv_s, cache_k, cache_v, layer):
    db, n_pages = page_table.shape
    t = mq_s.shape[0] // db
    past = n_pages * PAGE_SIZE
    w = MOBA_HEADS * MOBA_DIM
    assert past % MOBA_BLOCK == 0 and past >= MOBA_BLOCK and t <= MOBA_BLOCK
    ck = jnp.transpose(cache_k, (0, 1, 3, 4, 2))
    cv = jnp.transpose(cache_v, (0, 1, 3, 4, 2))
    seq_block = pl.BlockSpec((None, t, w), lambda b, pt: (b, 0, 0))
    grid_spec = pltpu.PrefetchScalarGridSpec(
        num_scalar_prefetch=1,
        grid=(db,),
        in_specs=[seq_block, seq_block, seq_block,
                  pl.BlockSpec(memory_space=pl.ANY), pl.BlockSpec(memory_space=pl.ANY)],
        out_specs=seq_block,
        scratch_shapes=[pltpu.VMEM((2, MOBA_HEADS, MOBA_DIM, past), F32),
                        pltpu.VMEM((2, MOBA_HEADS, MOBA_DIM, past), F32),
                        pltpu.SemaphoreType.DMA((2,))])
    r3 = lambda a: a.reshape(db, t, w)
    return pl.pallas_call(
        functools.partial(_moba_sample_kernel, layer=layer, n_pages=n_pages),
        grid_spec=grid_spec,
        out_shape=jax.ShapeDtypeStruct((db, t, w), BF16),
        compiler_params=_cparams(("arbitrary",)),
        name="moba_sample",
    )(page_table, r3(mq_s), r3(mk_s), r3(mv_s), ck, cv).reshape(db * t, w)


def _merge_kernel(x_ref, a_ref, b_ref, c_ref, g_ref, wpa_ref, wpb_ref, wpc_ref, wout_ref, gffn_ref, wq_ref,
                  xo_ref, hnt_ref, qp_ref):
    d = x_ref.shape[1]
    m = (g_ref[:, 0:d].astype(F32) * _dot(a_ref[...], wpa_ref[...])
         + g_ref[:, d:2 * d].astype(F32) * _dot(b_ref[...], wpb_ref[...])
         + g_ref[:, 2 * d:3 * d].astype(F32) * _dot(c_ref[...], wpc_ref[...]))
    x = x_ref[...] + _dot(m.astype(BF16), wout_ref[...])
    xo_ref[...] = x
    hn32 = _rms(x, gffn_ref[...])
    hn = hn32.astype(BF16)
    hnt_ref[...] = hn32.T.astype(BF16)
    wq_h = qp_ref.shape[2]
    for h in range(qp_ref.shape[0]):
        qp_ref[h] = _dot(hn, wq_ref[:, h * wq_h:(h + 1) * wq_h])


def _merge_call(x_all, a_all, b_all, c_all, gates, wpa_pad, wpb, wpc, wout, g_ffn, wq):
    t, d = x_all.shape
    tm = TOK_TILE
    row = lambda w: pl.BlockSpec((tm, w), lambda i: (i, 0))
    full = lambda a: pl.BlockSpec(a.shape, lambda i: (0,) * a.ndim)
    wq_h = wq.shape[1] // PEER_HEADS
    return pl.pallas_call(
        _merge_kernel,
        grid=(t // tm,),
        in_specs=[row(d), row(a_all.shape[1]), row(b_all.shape[1]), row(c_all.shape[1]), row(3 * d),
                  full(wpa_pad), full(wpb), full(wpc), full(wout), full(g_ffn), full(wq)],
        out_specs=[row(d), pl.BlockSpec((d, tm), lambda i: (0, i)),
                   pl.BlockSpec((PEER_HEADS, tm, wq_h), lambda i: (0, i, 0))],
        out_shape=[jax.ShapeDtypeStruct((t, d), F32), jax.ShapeDtypeStruct((d, t), BF16),
                   jax.ShapeDtypeStruct((PEER_HEADS, t, wq_h), F32)],
        compiler_params=_cparams(("parallel",)),
        name="merge",
    )(x_all, a_all, b_all, c_all, gates, wpa_pad, wpb, wpc, wout, g_ffn, wq)


_TAKEN = 2.0 ** 126


def _extract_rounds(s, order, exact):
    vals = []
    for r in range(PEER_TOPK):
        m = jnp.max(s, axis=0, keepdims=True)
        hit = s == m
        if exact:
            idx = jnp.min(jnp.where(hit, order, 1e9), axis=0, keepdims=True)
            hit = order == idx
        s = jnp.where(hit, -_TAKEN * (1.0 + r / PEER_TOPK), s)
        vals.append(m)
    return jnp.concatenate(vals, axis=0), s


def _is_taken(marked):
    return (marked <= -_TAKEN) & (marked > -jnp.inf)


def _extract_many(arrays, order):
    fast = [_extract_rounds(s, order, exact=False) for s in arrays]
    clean = None
    for _, marked in fast:
        n_taken = jnp.sum(jnp.where(_is_taken(marked), 1.0, 0.0), axis=0, keepdims=True)
        c = jnp.min(jnp.where(n_taken == float(PEER_TOPK), 1.0, 0.0))
        clean = c if clean is None else jnp.minimum(clean, c)
    flat = lax.cond(clean > 0.5,
                    lambda: tuple(x for pair in fast for x in pair),
                    lambda: tuple(x for s in arrays for x in _extract_rounds(s, order, exact=True)))
    return [(flat[2 * i], flat[2 * i + 1]) for i in range(len(arrays))]


def _select_heads(qs, keys):
    k = PEER_TOPK
    ka = 4
    tl = qs[0].shape[0]
    half = qs[0].shape[1] // 2
    scores = []
    for q, (k0, k1) in zip(qs, keys):
        scores += [_dot3_nt(k0, q[:, :half]), _dot3_nt(k1, q[:, half:])]
    rows = lax.broadcasted_iota(jnp.int32, scores[0].shape, 0).astype(F32)
    level1 = _extract_many(scores, rows)

    pos = lax.broadcasted_iota(jnp.int32, (ka * k, tl), 0)
    hi, lo = pos // k, pos % k
    ok_a = (hi + 1) * (lo + 1) <= k
    ok_b = ok_a & (lo >= ka)
    order = jnp.concatenate([jnp.where(ok_a, pos, 1000 + pos), jnp.where(ok_b, lo * k + hi, 2000 + pos)],
                            axis=0).astype(F32)
    cands = []
    for h in range(len(qs)):
        sv0, sv1 = level1[2 * h][0], level1[2 * h + 1][0]
        cand_a = (sv0[:ka, None, :] + sv1[None, :, :]).reshape(ka * k, tl)
        cand_b = (sv0[None, :, :] + sv1[:ka, None, :]).reshape(ka * k, tl)
        cands.append(jnp.concatenate([jnp.where(ok_a, cand_a, -jnp.inf), jnp.where(ok_b, cand_b, -jnp.inf)], axis=0))
    level2 = _extract_many(cands, order)

    out = []
    for h in range(len(qs)):
        (sv0, marked0), (sv1, marked1) = level1[2 * h], level1[2 * h + 1]
        top_s, marked = level2[h]
        rank0, rank1 = [jnp.where(m <= -_TAKEN, (m * (-1.0 / _TAKEN) - 1.0) * k, float(k)) for m in (marked0, marked1)]
        z = jnp.sum(jnp.exp(top_s - top_s[0:1, :]), axis=0, keepdims=True)
        picked = jnp.where(_is_taken(marked), 1.0, 0.0)
        cnt_a = jnp.sum(picked[:ka * k].reshape(ka, k, tl), axis=1)
        cnt_b = jnp.sum(picked[ka * k:].reshape(ka, k, tl), axis=0)
        count = jnp.zeros(rank0.shape, F32)
        for r in range(k):
            cr = cnt_a[r:r + 1, :] if r < ka else cnt_b[r:r + 1, :]
            count = count + jnp.where(rank0 == float(r), cr, 0.0)
        s0, s1 = scores[2 * h], scores[2 * h + 1]
        out.append((count, jnp.exp(s0 - sv0[0:1, :]), rank1.astype(BF16),
                    (jnp.exp(s1 - sv1[0:1, :]) / z).astype(BF16)))
    return out


def _peer_gate_kernel(q_ref, keys_ref, g_ref, row_ref, map_ref):
    heads, tl = q_ref.shape[0], q_ref.shape[1]
    n_keys = keys_ref.shape[2]

    def per_head_group(hg, carry):
        hs = [hg * SEL_HEAD_GROUP + k for k in range(SEL_HEAD_GROUP)]
        maps = _select_heads([q_ref[h] for h in hs], [(keys_ref[h, 0], keys_ref[h, 1]) for h in hs])
        for h, (count, a, rank1, b) in zip(hs, maps):
            row_ref[h, 0] = count
            row_ref[h, 1] = a
            map_ref[h, 0] = rank1
            map_ref[h, 1] = b
        return carry

    lax.fori_loop(0, heads // SEL_HEAD_GROUP, per_head_group, 0)

    sub = _BF16_SUBLANES
    shape3 = (n_keys // sub, sub, tl)

    def per_row(i, carry):
        gate = jnp.zeros(shape3, BF16)
        for h in range(heads):
            cnt = jnp.broadcast_to(row_ref[h, 0, pl.ds(i, 1), :], (sub, tl)).astype(BF16)
            a = jnp.broadcast_to(row_ref[h, 1, pl.ds(i, 1), :], (sub, tl)).astype(BF16)
            b = map_ref[h, 1].reshape(shape3)
            gate = gate + jnp.where(map_ref[h, 0].reshape(shape3) < cnt[None], a[None] * b, jnp.zeros_like(b))
        g_ref[pl.ds(pl.multiple_of(i * n_keys, n_keys), n_keys), :] = gate.reshape(n_keys, tl)
        return carry

    lax.fori_loop(0, n_keys, per_row, 0)


def _peer_gate_call(qp, keys):
    heads, t, wq_h = qp.shape
    _, _, n_keys, half = keys.shape
    tl = SEL_TILE
    return pl.pallas_call(
        _peer_gate_kernel,
        grid=(t // tl,),
        in_specs=[pl.BlockSpec((heads, tl, wq_h), lambda i: (0, i, 0)),
                  pl.BlockSpec(keys.shape, lambda i: (0, 0, 0, 0))],
        out_specs=pl.BlockSpec((n_keys * n_keys, tl), lambda i: (0, i)),
        out_shape=jax.ShapeDtypeStruct((n_keys * n_keys, t), BF16),
        scratch_shapes=[pltpu.VMEM((heads, 2, n_keys, tl), F32), pltpu.VMEM((heads, 2, n_keys, tl), BF16)],
        compiler_params=_cparams(("parallel",)),
        name="peer_gate",
    )(qp, keys)


def _peer_expert_kernel(hnt_ref, x_ref, g_ref, u_ref, vt_ref, gfin_ref, o_ref, acc_ref, *, final_norm):
    e = pl.program_id(1)

    @pl.when(e == 0)
    def _():
        acc_ref[...] = jnp.zeros_like(acc_ref)

    act = _gelu_tanh(_dot(u_ref[...], hnt_ref[...])).astype(BF16)
    acc_ref[...] += _dot(vt_ref[...], g_ref[...] * act)

    @pl.when(e == pl.num_programs(1) - 1)
    def _():
        x = x_ref[...] + acc_ref[...].T
        if final_norm:
            x = _rms(x, gfin_ref[...])
        o_ref[...] = x


def _peer_expert_call(hnt, x_all, gates, u_tab, v_tab_t, g_final, final_norm):
    t, d = x_all.shape
    tt = PEER_TOK_TILE
    te = PEER_EXPERT_TILE
    return pl.pallas_call(
        functools.partial(_peer_expert_kernel, final_norm=final_norm),
        grid=(t // tt, u_tab.shape[0] // te),
        in_specs=[pl.BlockSpec((d, tt), lambda i, e: (0, i)),
                  pl.BlockSpec((tt, d), lambda i, e: (i, 0)),
                  pl.BlockSpec((te, tt), lambda i, e: (e, i)),
                  pl.BlockSpec((te, d), lambda i, e: (e, 0)),
                  pl.BlockSpec((d, te), lambda i, e: (0, e)),
                  pl.BlockSpec((1, d), lambda i, e: (0, 0))],
        out_specs=pl.BlockSpec((tt, d), lambda i, e: (i, 0)),
        out_shape=jax.ShapeDtypeStruct((t, d), F32),
        scratch_shapes=[pltpu.VMEM((d, tt), F32)],
        compiler_params=_cparams(("parallel", "arbitrary")),
        name="peer_expert",
    )(hnt, x_all, gates, u_tab, v_tab_t, g_final)


def kernel(x_prompt, x_sample, cache_mla_latent, cache_mla_krope, cache_moba_k, cache_moba_v, page_table,
           g_mix, w_in, g_q_lat, w_uq, g_kv_lat, w_ukv, g_gm_v, b_gm_v, w_s, b_s,
           w_pa, w_pb, w_pc, w_out, g_ffn, w_peer_q, peer_keys, peer_u, peer_v, g_final):
    batch, seq, d = x_prompt.shape
    db, t_new, _ = x_sample.shape
    depth = w_in.shape[0]
    n_pages = page_table.shape[1]
    past = n_pages * PAGE_SIZE
    n_p, n_s = batch * seq, db * t_new
    assert n_p % TOK_TILE == 0 and n_s % TOK_TILE == 0 and GM_CHUNK % t_new == 0
    assert (n_p + n_s) % PEER_TOK_TILE == 0 and seq % ATT_TILE == 0

    pos = jnp.concatenate([jnp.tile(jnp.arange(seq, dtype=jnp.int32), batch),
                           jnp.tile(past + jnp.arange(t_new, dtype=jnp.int32), db)])
    tabs = _rope_tables(pos)
    x_all = jnp.concatenate([x_prompt.reshape(n_p, d), x_sample.reshape(n_s, d)], axis=0)
    row2 = lambda a: a.reshape(1, -1)
    kvl = w_ukv.shape[1]
    cache_pe_t = jnp.transpose(cache_mla_krope, (0, 1, 3, 2))

    outs = {k: [] for k in ("lat", "pe", "mk", "mv", "gv")}
    for l in range(depth):
        w_arr = _arrange_w_in(w_in[l])
        wuq, wuqr = _arrange_w_uq(w_uq[l])
        reps = GM_CHUNK // t_new
        w_small = w_s[l][:, :t_new, :t_new]
        eye = jnp.eye(reps, dtype=w_s.dtype)
        w_samp = jnp.einsum("ab,gts->gatbs", eye, w_small).reshape(GM_GROUPS, GM_CHUNK, GM_CHUNK)
        wmix = jnp.stack([w_s[l], w_samp])
        b_full = jnp.repeat(b_s[l].T, GM_DIM, axis=1)
        b_samp = jnp.tile(jnp.repeat(b_s[l][:, :t_new].T, GM_DIM, axis=1), (reps, 1))
        bmix = jnp.stack([b_full, b_samp])
        w3 = w_ukv[l].reshape(w_ukv.shape[1], MLA_HEADS, MLA_NOPE + MLA_V)
        wuk_pad = jnp.concatenate(
            [jnp.transpose(w3[..., :MLA_NOPE], (1, 2, 0)),
             jnp.zeros((MLA_HEADS, LANES - MLA_NOPE, w3.shape[0]), w3.dtype)], axis=1).astype(BF16)
        wuv_pad = jnp.concatenate(
            [jnp.zeros((MLA_HEADS, w3.shape[0], LANES - MLA_V), w3.dtype),
             jnp.transpose(w3[..., MLA_NOPE:], (1, 0, 2))], axis=2).astype(BF16)
        wpa3 = w_pa[l].reshape(MLA_HEADS, MLA_V, d)
        wpa_pad = jnp.concatenate([jnp.zeros((MLA_HEADS, LANES - MLA_V, d), w_pa.dtype), wpa3],
                                  axis=1).reshape(MLA_HEADS * LANES, d).astype(BF16)

        (ckv, kpe, mk, mv, gv, qall, kpep, mq, c_all, gates, ckvt, mvt) = _proj_call(
            x_all, tabs, n_p // TOK_TILE, row2(g_mix[l]), w_arr, row2(g_q_lat[l]), wuq, wuqr,
            row2(g_kv_lat[l]), row2(g_gm_v[l]), row2(b_gm_v[l]), wmix, bmix)

        a_p = _mla_prompt_call(qall, ckv, kpep, ckvt, wuk_pad, wuv_pad, batch, seq)
        b_p = _moba_prompt_call(mq, mk, mvt, batch, seq)
        a_s = _mla_sample_call(page_table, qall[n_p:], ckv[n_p:], kpep[n_p:], wuk_pad, wuv_pad,
                               cache_mla_latent, cache_pe_t, l)
        b_s_out = _moba_sample_call(page_table, mq[n_p:], mk[n_p:], mv[n_p:], cache_moba_k, cache_moba_v, l)
        a_all = jnp.concatenate([a_p, a_s], axis=0)
        b_all = jnp.concatenate([b_p, b_s_out], axis=0)

        x_mid, hnt, qp = _merge_call(x_all, a_all, b_all, c_all, gates, wpa_pad, w_pb[l].astype(BF16),
                                     w_pc[l].astype(BF16), w_out[l].astype(BF16), row2(g_ffn[l]),
                                     w_peer_q[l].astype(BF16))
        peer_gates = _peer_gate_call(qp, peer_keys[l])
        x_all = _peer_expert_call(hnt, x_mid, peer_gates, peer_u[l].astype(BF16),
                                  peer_v[l].T.astype(BF16), row2(g_final), l == depth - 1)

        outs["lat"].append(ckv)
        outs["pe"].append(kpe)
        outs["mk"].append(mk)
        outs["mv"].append(mv)
        outs["gv"].append(gv)

    def split(name, tail):
        st = jnp.stack(outs[name])
        return (st[:, :n_p].reshape((depth, batch, seq) + tail), st[:, n_p:].reshape((depth, db, t_new) + tail))

    lat_p, lat_s = split("lat", (kvl,))
    pe_p, pe_s = split("pe", (MLA_ROPE,))
    mk_p, mk_s = split("mk", (MOBA_HEADS, MOBA_DIM))
    mv_p, mv_s = split("mv", (MOBA_HEADS, MOBA_DIM))
    _, gv_s = split("gv", (GM_GROUPS, GM_DIM))
    y_prompt = x_all[:n_p].reshape(batch, seq, d)
    y_sample = x_all[n_p:].reshape(db, t_new, d)
    return (y_prompt, y_sample, lat_p, pe_p, mk_p, mv_p, lat_s, pe_s, mk_s, mv_s, gv_s)
```

```python
import functools

import jax
import jax.numpy as jnp
import numpy as np
from jax import lax
from jax.experimental import pallas as pl
from jax.experimental.pallas import tpu as pltpu

F32 = jnp.float32
BF16 = jnp.bfloat16

MLA_HEADS = 8
MLA_NOPE = 64
MLA_ROPE = 32
MLA_V = 64
MOBA_HEADS = 4
MOBA_DIM = 64
MOBA_BLOCK = 256
MOBA_TOPK = 3
GM_GROUPS = 4
GM_DIM = 64
GM_CHUNK = 128
PEER_HEADS = 8
PEER_TOPK = 16
PAGE_SIZE = 128
ROPE_THETA = 10000.0
EPS = 1e-6
NEG_INF = -1e30

LANES = 128
_BF16_SUBLANES = 16
VMEM_LIMIT = 56 * 1024 * 1024

TOK_TILE = 256
ATT_TILE = 256
SEL_TILE = 256
SEL_HEAD_GROUP = 4
PEER_TOK_TILE = 1024
PEER_EXPERT_TILE = 1024

_NT = (((1,), (1,)), ((), ()))


def _cparams(sem):
    return pltpu.CompilerParams(dimension_semantics=sem, vmem_limit_bytes=VMEM_LIMIT)


def _dot(a, b):
    return jnp.dot(a, b, preferred_element_type=F32)


def _dot_nt(a, b):
    return lax.dot_general(a, b, _NT, preferred_element_type=F32)


def _split(a):
    hi = a.astype(BF16)
    lo = (a - hi.astype(F32)).astype(BF16)
    return hi, lo


def _dot3_nt(a, b):
    ah, al = _split(a)
    bh, bl = _split(b)
    return _dot_nt(ah, bh) + _dot_nt(ah, bl) + _dot_nt(al, bh)


def _rms(x, g):
    return x * lax.rsqrt(jnp.mean(x * x, axis=-1, keepdims=True) + EPS) * g


def _gelu_tanh(x):
    c = float(np.sqrt(2.0 / np.pi))
    u = x * ((x * x) * (c * 0.044715) + c)
    hx = 0.5 * x
    return hx * jnp.tanh(u) + hx


def _lane_group(shape, width):
    return lax.broadcasted_iota(jnp.int32, shape, len(shape) - 1) // width


_C_QLAT, _C_KVLAT, _C_KPE, _C_KPER = 0, 256, 512, 640
_C_MQ, _C_MQR, _C_MK, _C_MKR, _C_MV, _C_U, _C_V, _C_G = 768, 1024, 1280, 1536, 1792, 2048, 2304, 2560


def _proj_kernel(x_ref, gmix_ref, w_ref, gq_ref, wuq_ref, wuqr_ref, gkv_ref,
                 ggv_ref, bgv_ref, wmix_ref, bmix_ref,
                 cosq_ref, sinq_ref, cosk_ref, sink_ref, cosm_ref, sinm_ref,
                 ckv_ref, kpe_ref, mk_ref, mv_ref, gv_ref,
                 qall_ref, kpep_ref, mq_ref, c_ref, gates_ref, ckvt_ref, mvt_ref):
    d_model = x_ref.shape[1]
    hb = _rms(x_ref[...], gmix_ref[...]).astype(BF16)

    def seg(off, width):
        return _dot(hb, w_ref[:, off:off + width])

    qn = _rms(seg(_C_QLAT, 256), gq_ref[...]).astype(BF16)
    cq, sq = cosq_ref[...], sinq_ref[...]
    for h in range(MLA_HEADS):
        sl = slice(h * 256, (h + 1) * 256)
        qa = _dot(qn, wuq_ref[:, sl])
        qr = _dot(qn, wuqr_ref[:, sl])
        qall_ref[:, sl] = (qa * cq + qr * sq).astype(BF16)

    ckv = _rms(seg(_C_KVLAT, 256), gkv_ref[...])
    ckv_ref[...] = ckv
    ckvt_ref[...] = ckv.T.astype(BF16)

    kpe = seg(_C_KPE, LANES) * cosk_ref[...] + seg(_C_KPER, LANES) * sink_ref[...]
    kpep_ref[...] = kpe.astype(BF16)
    kpe_ref[...] = kpe[:, :MLA_ROPE]

    cm, sm = cosm_ref[...], sinm_ref[...]
    mq_ref[...] = seg(_C_MQ, 256) * cm[:, :256] + seg(_C_MQR, 256) * sm[:, :256]
    mk_ref[...] = seg(_C_MK, 256) * cm[:, 256:] + seg(_C_MKR, 256) * sm[:, 256:]
    mv = seg(_C_MV, 256)
    mv_ref[...] = mv
    mvt_ref[...] = mv.T.astype(BF16)

    u = jax.nn.gelu(seg(_C_U, 256))
    gvx = jax.nn.gelu(seg(_C_V, 256))
    mu = jnp.mean(gvx, axis=-1, keepdims=True)
    xc = gvx - mu
    v = xc * lax.rsqrt(jnp.mean(xc * xc, axis=-1, keepdims=True) + EPS) * ggv_ref[...] + bgv_ref[...]
    gv_ref[...] = v
    tm = x_ref.shape[0]
    row = lax.broadcasted_iota(jnp.int32, (GM_CHUNK, GM_CHUNK), 0)
    col = lax.broadcasted_iota(jnp.int32, (GM_CHUNK, GM_CHUNK), 1)
    grp = _lane_group((GM_CHUNK, GM_GROUPS * GM_DIM), GM_DIM)
    for ci in range(tm // GM_CHUNK):
        rs = slice(ci * GM_CHUNK, (ci + 1) * GM_CHUNK)
        vc = v[rs]
        mixed = bmix_ref[...]
        for g in range(GM_GROUPS):
            wg = jnp.where(col <= row, wmix_ref[g], 0.0).astype(BF16)
            vg = jnp.where(grp == g, vc, 0.0).astype(BF16)
            mixed = mixed + _dot(wg, vg)
        c_ref[rs, :] = (u[rs] * mixed).astype(BF16)

    for j in range(3):
        gates_ref[:, j * d_model:(j + 1) * d_model] = jax.nn.sigmoid(
            seg(_C_G + j * d_model, d_model)).astype(BF16)


def _rot_cols(w, d):
    k, n = w.shape
    w3 = w.reshape(k, n // d, d)
    return jnp.concatenate([-w3[..., d // 2:], w3[..., :d // 2]], axis=-1).reshape(k, n)


def _rope_tables(pos):
    pos = pos.astype(F32)[:, None]

    def cs(d):
        inv = ROPE_THETA ** (-jnp.arange(0, d, 2, dtype=F32) / d)
        ang = pos * inv[None, :]
        return (jnp.concatenate([jnp.cos(ang)] * 2, axis=1), jnp.concatenate([jnp.sin(ang)] * 2, axis=1))

    n = pos.shape[0]
    c32, s32 = cs(MLA_ROPE)
    c64, s64 = cs(MOBA_DIM)
    ones, zeros = jnp.ones((n, LANES), F32), jnp.zeros((n, LANES), F32)
    pad = jnp.zeros((n, LANES - MLA_ROPE), F32)
    mla_scale = (MLA_NOPE + MLA_ROPE) ** -0.5
    cosq = jnp.concatenate([ones, c32, pad], axis=1) * mla_scale
    sinq = jnp.concatenate([zeros, s32, pad], axis=1) * mla_scale
    cosk = jnp.concatenate([c32, pad], axis=1)
    sink = jnp.concatenate([s32, pad], axis=1)
    moba_scale = MOBA_DIM ** -0.5
    c64h, s64h = jnp.tile(c64, (1, MOBA_HEADS)), jnp.tile(s64, (1, MOBA_HEADS))
    cosm = jnp.concatenate([c64h * moba_scale, c64h], axis=1)
    sinm = jnp.concatenate([s64h * moba_scale, s64h], axis=1)
    return cosq, sinq, cosk, sink, cosm, sinm


def _arrange_w_in(w_in):
    d = w_in.shape[0]
    o = np.cumsum([0, 256, 256, MLA_ROPE, 256, 256, 256, 256, 256])
    q_lat, kv_lat, k_rope, m_q, m_k, m_v, g_u, g_v = (w_in[:, o[i]:o[i + 1]] for i in range(8))
    gates = w_in[:, o[8]:]
    padk = jnp.zeros((d, LANES - MLA_ROPE), w_in.dtype)
    cols = [q_lat, kv_lat, k_rope, padk, _rot_cols(k_rope, MLA_ROPE), padk,
            m_q, _rot_cols(m_q, MOBA_DIM), m_k, _rot_cols(m_k, MOBA_DIM), m_v, g_u, g_v, gates]
    return jnp.concatenate(cols, axis=1).astype(BF16)


def _arrange_w_uq(w_uq):
    k = w_uq.shape[0]
    w3 = w_uq.reshape(k, MLA_HEADS, MLA_NOPE + MLA_ROPE)
    nope, pe = w3[..., :MLA_NOPE], w3[..., MLA_NOPE:]
    z64 = jnp.zeros((k, MLA_HEADS, LANES - MLA_NOPE), w_uq.dtype)
    z96 = jnp.zeros((k, MLA_HEADS, LANES - MLA_ROPE), w_uq.dtype)
    z128 = jnp.zeros((k, MLA_HEADS, LANES), w_uq.dtype)
    pe_rot = jnp.concatenate([-pe[..., MLA_ROPE // 2:], pe[..., :MLA_ROPE // 2]], axis=-1)
    big = jnp.concatenate([nope, z64, pe, z96], axis=-1).reshape(k, MLA_HEADS * 256)
    big_rot = jnp.concatenate([z128, pe_rot, z96], axis=-1).reshape(k, MLA_HEADS * 256)
    return big.astype(BF16), big_rot.astype(BF16)


def _proj_call(x_all, tabs, n_prompt_tiles, g_mix, w_arr, g_q, wuq, wuqr, g_kv, g_gv, b_gv, wmix, bmix):
    t, d = x_all.shape
    tm = TOK_TILE
    nt = t // tm
    row = lambda w: pl.BlockSpec((tm, w), lambda i: (i, 0))
    full = lambda a: pl.BlockSpec(a.shape, lambda i: (0,) * a.ndim)
    kind = lambda i: jnp.where(i >= n_prompt_tiles, 1, 0)
    in_specs = [row(d), full(g_mix), full(w_arr), full(g_q), full(wuq), full(wuqr), full(g_kv),
                full(g_gv), full(b_gv),
                pl.BlockSpec((None, GM_GROUPS, GM_CHUNK, GM_CHUNK), lambda i: (kind(i), 0, 0, 0)),
                pl.BlockSpec((None, GM_CHUNK, GM_GROUPS * GM_DIM), lambda i: (kind(i), 0, 0)),
                row(256), row(256), row(LANES), row(LANES), row(512), row(512)]
    outs = [((t, 256), F32), ((t, MLA_ROPE), F32), ((t, 256), F32), ((t, 256), F32), ((t, 256), F32),
            ((t, MLA_HEADS * 256), BF16), ((t, LANES), BF16),
            ((t, 256), F32), ((t, 256), BF16), ((t, 3 * d), BF16)]
    col = lambda h: pl.BlockSpec((None, h, tm), lambda i: (i, 0, 0))
    outs_t = [(nt, g_kv.shape[1], tm), (nt, MOBA_HEADS * MOBA_DIM, tm)]
    return pl.pallas_call(
        _proj_kernel,
        grid=(nt,),
        in_specs=in_specs,
        out_specs=[row(s[1]) for s, _ in outs] + [col(s[1]) for s in outs_t],
        out_shape=[jax.ShapeDtypeStruct(s, dt) for s, dt in outs] + [jax.ShapeDtypeStruct(s, BF16) for s in outs_t],
        compiler_params=_cparams(("parallel",)),
        name="proj",
    )(x_all, g_mix, w_arr, g_q, wuq, wuqr, g_kv, g_gv, b_gv, wmix, bmix, *tabs)


def _mla_prompt_kernel(q_ref, lat_ref, kpe_ref, latt_ref, wuk_ref, wuv_ref, o_ref, acc_ref):
    tq = q_ref.shape[0]
    qi = pl.program_id(1)
    q = q_ref[...]
    qs = []
    for h in range(MLA_HEADS):
        qa = _dot(q[:, h * 256:h * 256 + LANES], wuk_ref[h]).astype(BF16)
        qs.append(jnp.concatenate([qa, q[:, h * 256 + LANES:(h + 1) * 256]], axis=1))
    qs = jnp.concatenate(qs, axis=0)
    n = qs.shape[0]

    def tile(j, carry, masked):
        m, l = carry
        rows = pl.ds(pl.multiple_of(j * tq, tq), tq)
        keys = jnp.concatenate([lat_ref[rows, :].astype(BF16), kpe_ref[rows, :]], axis=1)
        s = _dot_nt(keys, qs)
        if masked:
            key = lax.broadcasted_iota(jnp.int32, s.shape, 0)
            qry = lax.broadcasted_iota(jnp.int32, s.shape, 1) % tq
            s = jnp.where(key <= qry, s, NEG_INF)
        m_new = jnp.maximum(m, jnp.max(s, axis=0, keepdims=True))
        alpha = jnp.exp(m - m_new)
        p = jnp.exp(s - m_new)
        l = alpha * l + jnp.sum(p, axis=0, keepdims=True)
        acc_ref[...] = alpha * acc_ref[...] + _dot(latt_ref[j], p.astype(BF16))
        return m_new, l

    acc_ref[...] = jnp.zeros_like(acc_ref)
    carry = tile(qi, (jnp.full((1, n), NEG_INF, F32), jnp.zeros((1, n), F32)), True)
    m, l = lax.fori_loop(0, qi, lambda j, c: tile(j, c, False), carry)
    o = acc_ref[...] / l
    for h in range(MLA_HEADS):
        o_h = o[:, h * tq:(h + 1) * tq].T.astype(BF16)
        o_ref[:, h * LANES:(h + 1) * LANES] = _dot(o_h, wuv_ref[h]).astype(BF16)


def _mla_prompt_call(qall, ckv, kpep, ckvt, wuk_pad, wuv_pad, batch, seq):
    tq = ATT_TILE
    nq = seq // tq
    kvl = ckv.shape[1]
    full = lambda a: pl.BlockSpec(a.shape, lambda b, i: (0,) * a.ndim)
    return pl.pallas_call(
        _mla_prompt_kernel,
        grid=(batch, nq),
        in_specs=[pl.BlockSpec((tq, MLA_HEADS * 256), lambda b, i: (b * nq + i, 0)),
                  pl.BlockSpec((seq, kvl), lambda b, i: (b, 0)),
                  pl.BlockSpec((seq, LANES), lambda b, i: (b, 0)),
                  pl.BlockSpec((nq, kvl, tq), lambda b, i: (b, 0, 0)),
                  full(wuk_pad), full(wuv_pad)],
        out_specs=pl.BlockSpec((tq, MLA_HEADS * LANES), lambda b, i: (b * nq + i, 0)),
        out_shape=jax.ShapeDtypeStruct((batch * seq, MLA_HEADS * LANES), BF16),
        scratch_shapes=[pltpu.VMEM((kvl, MLA_HEADS * tq), F32)],
        compiler_params=_cparams(("parallel", "arbitrary")),
        name="mla_prompt",
    )(qall, ckv, kpep, ckvt, wuk_pad, wuv_pad)


def _topk_bias(gate, n_valid, k):
    nblk = gate.shape[1]
    blk = lax.broadcasted_iota(jnp.int32, gate.shape, 1)
    valid = blk < n_valid
    g = jnp.where(valid, gate, NEG_INF)
    rank = jnp.zeros(gate.shape, F32)
    for n in range(nblk):
        gn = g[:, n:n + 1]
        ahead = (gn > g) | ((gn == g) & (n < blk))
        rank = rank + jnp.where(ahead, 1.0, 0.0)
    return jnp.where(valid & (rank < k), 0.0, NEG_INF)


def _stack_heads(q, heads, width):
    grp = _lane_group(q.shape, width)
    return jnp.concatenate([jnp.where(grp == h, q, jnp.zeros_like(q)) for h in range(heads)], axis=0)


def _unstack_heads(o, heads, width):
    rows = o.shape[0] // heads
    grp = _lane_group((rows, o.shape[1]), width)
    out = jnp.zeros((rows, o.shape[1]), F32)
    for h in range(heads):
        out = out + jnp.where(grp == h, o[h * rows:(h + 1) * rows], 0.0)
    return out


def _topk_bias_t(gate_t, n_valid, k):
    nblk = gate_t.shape[0]
    blk = lax.broadcasted_iota(jnp.int32, gate_t.shape, 0)
    valid = blk < n_valid
    g = jnp.where(valid, gate_t, NEG_INF)
    rank = jnp.zeros(gate_t.shape, F32)
    for n in range(nblk):
        gn = g[n:n + 1, :]
        ahead = (gn > g) | ((gn == g) & (n < blk))
        rank = rank + jnp.where(ahead, 1.0, 0.0)
    return jnp.where(valid & (rank < k), 0.0, NEG_INF)


def _moba_prompt_kernel(q_ref, k_ref, vt_ref, o_ref, kmean_ref, bias_ref, *, ksel):
    tq = q_ref.shape[0]
    nblk = k_ref.shape[0] // MOBA_BLOCK
    qi = pl.program_id(1)

    @pl.when(qi == 0)
    def _():
        for n in range(nblk):
            kmean_ref[n:n + 1, :] = jnp.mean(k_ref[n * MOBA_BLOCK:(n + 1) * MOBA_BLOCK, :], axis=0, keepdims=True)

    qf = _stack_heads(q_ref[...], MOBA_HEADS, MOBA_DIM)
    q = qf.astype(BF16)
    bias_ref[...] = _topk_bias_t(_dot3_nt(kmean_ref[...], qf), qi, ksel)

    def tile(j, carry, own):
        m, l, acc = carry
        rows = pl.ds(pl.multiple_of(j * MOBA_BLOCK, MOBA_BLOCK), MOBA_BLOCK)
        s = _dot_nt(k_ref[rows, :].astype(BF16), q)
        if own:
            key = lax.broadcasted_iota(jnp.int32, s.shape, 0)
            qry = lax.broadcasted_iota(jnp.int32, s.shape, 1) % tq
            s = jnp.where(key <= qry, s, NEG_INF)
        else:
            s = s + bias_ref[pl.ds(j, 1), :]
        m_new = jnp.maximum(m, jnp.max(s, axis=0, keepdims=True))
        alpha = jnp.exp(m - m_new)
        p = jnp.exp(s - m_new)
        l = alpha * l + jnp.sum(p, axis=0, keepdims=True)
        acc = alpha * acc + _dot(vt_ref[j], p.astype(BF16))
        return m_new, l, acc

    n = qf.shape[0]
    w = qf.shape[1]
    init = (jnp.full((1, n), NEG_INF, F32), jnp.zeros((1, n), F32), jnp.zeros((w, n), F32))
    carry = tile(qi, init, True)
    m, l, acc = lax.fori_loop(0, qi, lambda j, c: tile(j, c, False), carry)
    o = acc / l
    grp = lax.broadcasted_iota(jnp.int32, (w, tq), 0) // MOBA_DIM
    out_t = jnp.zeros((w, tq), F32)
    for h in range(MOBA_HEADS):
        out_t = out_t + jnp.where(grp == h, o[:, h * tq:(h + 1) * tq], 0.0)
    o_ref[...] = out_t.T.astype(BF16)


def _moba_prompt_call(mq, mk, mvt, batch, seq):
    assert ATT_TILE == MOBA_BLOCK == TOK_TILE and seq % MOBA_BLOCK == 0
    w = MOBA_HEADS * MOBA_DIM
    nq = seq // ATT_TILE
    ksel = min(MOBA_TOPK, (seq - 1) // MOBA_BLOCK)
    return pl.pallas_call(
        functools.partial(_moba_prompt_kernel, ksel=ksel),
        grid=(batch, nq),
        in_specs=[pl.BlockSpec((ATT_TILE, w), lambda b, i: (b * nq + i, 0)),
                  pl.BlockSpec((seq, w), lambda b, i: (b, 0)),
                  pl.BlockSpec((nq, w, ATT_TILE), lambda b, i: (b, 0, 0))],
        out_specs=pl.BlockSpec((ATT_TILE, w), lambda b, i: (b * nq + i, 0)),
        out_shape=jax.ShapeDtypeStruct((batch * seq, w), BF16),
        scratch_shapes=[pltpu.VMEM((nq, w), F32), pltpu.VMEM((nq, MOBA_HEADS * ATT_TILE), F32)],
        compiler_params=_cparams(("parallel", "arbitrary")),
        name="moba_prompt",
    )(mq, mk, mvt)


def _page_copies(pt_ref, seq_idx, layer, n_pages, slot, pairs, sem):
    copies = []
    for p in range(n_pages):
        page = pt_ref[seq_idx, p]
        rows = pl.ds(p * PAGE_SIZE, PAGE_SIZE)
        for cache, buf, rows_last in pairs:
            mid = (slice(None),) * (len(buf.shape) - 2)
            dst = buf.at[(slot,) + mid + (rows,)] if rows_last else buf.at[slot, rows, :]
            copies.append(pltpu.make_async_copy(cache.at[layer, page], dst, sem.at[slot]))
    return copies


def _gather_pages(pt_ref, layer, n_pages, pairs, sem):
    b = pl.program_id(0)
    nb = pl.num_programs(0)
    slot = b % 2

    @pl.when(b == 0)
    def _():
        for c in _page_copies(pt_ref, b, layer, n_pages, slot, pairs, sem):
            c.start()

    @pl.when(b + 1 < nb)
    def _():
        for c in _page_copies(pt_ref, b + 1, layer, n_pages, 1 - slot, pairs, sem):
            c.start()

    for c in _page_copies(pt_ref, b, layer, n_pages, slot, pairs, sem):
        c.wait()
    return slot


def _mla_sample_kernel(pt_ref, q_ref, ckv_ref, kpep_ref, wuk_ref, wuv_ref, lat_hbm, pe_hbm, o_ref,
                       lat_buf, pe_buf, sem, *, layer, n_pages):
    slot = _gather_pages(pt_ref, layer, n_pages, [(lat_hbm, lat_buf, False), (pe_hbm, pe_buf, True)], sem)
    t = q_ref.shape[0]
    q = q_ref[...]
    qa, qp = [], []
    for h in range(MLA_HEADS):
        qa.append(_dot(q[:, h * 256:h * 256 + LANES], wuk_ref[h]))
        qp.append(q[:, h * 256 + LANES:h * 256 + LANES + MLA_ROPE].astype(F32))
    qa = jnp.concatenate(qa, axis=0).astype(BF16)
    qp = jnp.concatenate(qp, axis=0).astype(BF16)
    lat = lat_buf[slot].astype(BF16)
    pe_t = pe_buf[slot].astype(BF16)
    s_p = _dot_nt(qa, lat) + _dot(qp, pe_t)
    ckv = ckv_ref[...].astype(BF16)
    kpn = kpep_ref[...][:, :MLA_ROPE]
    s_n = _dot_nt(qa, ckv) + _dot_nt(qp, kpn)
    r = lax.broadcasted_iota(jnp.int32, s_n.shape, 0) % t
    c = lax.broadcasted_iota(jnp.int32, s_n.shape, 1)
    s_n = jnp.where(c <= r, s_n, NEG_INF)
    m = jnp.maximum(jnp.max(s_p, axis=-1, keepdims=True), jnp.max(s_n, axis=-1, keepdims=True))
    p_p = jnp.exp(s_p - m)
    p_n = jnp.exp(s_n - m)
    l = jnp.sum(p_p, axis=-1, keepdims=True) + jnp.sum(p_n, axis=-1, keepdims=True)
    o_lat = (_dot(p_p.astype(BF16), lat) + _dot(p_n.astype(BF16), ckv)) / l
    for h in range(MLA_HEADS):
        o_ref[:, h * LANES:(h + 1) * LANES] = _dot(o_lat[h * t:(h + 1) * t].astype(BF16), wuv_ref[h]).astype(BF16)


def _mla_sample_call(page_table, qall_s, ckv_s, kpep_s, wuk_pad, wuv_pad, cache_lat, cache_pe, layer):
    db, n_pages = page_table.shape
    t = qall_s.shape[0] // db
    past = n_pages * PAGE_SIZE
    kvl = cache_lat.shape[-1]
    q3 = qall_s.reshape(db, t, MLA_HEADS * 256)
    c3 = ckv_s.reshape(db, t, kvl)
    k3 = kpep_s.reshape(db, t, LANES)
    seq_block = lambda w: pl.BlockSpec((None, t, w), lambda b, pt: (b, 0, 0))
    full = lambda a: pl.BlockSpec(a.shape, lambda b, pt: (0,) * a.ndim)
    grid_spec = pltpu.PrefetchScalarGridSpec(
        num_scalar_prefetch=1,
        grid=(db,),
        in_specs=[seq_block(MLA_HEADS * 256), seq_block(kvl), seq_block(LANES), full(wuk_pad), full(wuv_pad),
                  pl.BlockSpec(memory_space=pl.ANY), pl.BlockSpec(memory_space=pl.ANY)],
        out_specs=seq_block(MLA_HEADS * LANES),
        scratch_shapes=[pltpu.VMEM((2, past, kvl), F32), pltpu.VMEM((2, MLA_ROPE, past), F32),
                        pltpu.SemaphoreType.DMA((2,))])
    return pl.pallas_call(
        functools.partial(_mla_sample_kernel, layer=layer, n_pages=n_pages),
        grid_spec=grid_spec,
        out_shape=jax.ShapeDtypeStruct((db, t, MLA_HEADS * LANES), BF16),
        compiler_params=_cparams(("arbitrary",)),
        name="mla_sample",
    )(page_table, q3, c3, k3, wuk_pad, wuv_pad, cache_lat, cache_pe).reshape(db * t, MLA_HEADS * LANES)


def _moba_sample_kernel(pt_ref, q_ref, kn_ref, vn_ref, k_hbm, v_hbm, o_ref, k_buf, v_buf, sem,
                        *, layer, n_pages):
    slot = _gather_pages(pt_ref, layer, n_pages, [(k_hbm, k_buf, True), (v_hbm, v_buf, True)], sem)
    t = q_ref.shape[0]
    past = n_pages * PAGE_SIZE
    nblk = past // MOBA_BLOCK
    w = q_ref.shape[1]
    q = _stack_heads(q_ref[...], MOBA_HEADS, MOBA_DIM).astype(BF16)
    k_t = k_buf[slot].reshape(w, past).astype(BF16)
    s_raw = _dot(q, k_t)
    blk = lax.broadcasted_iota(jnp.int32, (q.shape[0], nblk), 1)
    gate = jnp.zeros((q.shape[0], nblk), F32)
    for n in range(nblk):
        gn = jnp.sum(s_raw[:, n * MOBA_BLOCK:(n + 1) * MOBA_BLOCK], axis=1, keepdims=True)
        gate = gate + jnp.where(blk == n, gn, 0.0)
    bias = _topk_bias(gate, nblk, min(MOBA_TOPK, nblk))
    expand = (lax.broadcasted_iota(jnp.int32, (nblk, past), 1) // MOBA_BLOCK
              == lax.broadcasted_iota(jnp.int32, (nblk, past), 0)).astype(BF16)
    s_p = s_raw + _dot(bias.astype(BF16), expand)
    kn = kn_ref[...].astype(BF16)
    s_n = _dot_nt(q, kn)
    r = lax.broadcasted_iota(jnp.int32, s_n.shape, 0) % t
    c = lax.broadcasted_iota(jnp.int32, s_n.shape, 1)
    s_n = jnp.where(c <= r, s_n, NEG_INF)
    m = jnp.maximum(jnp.max(s_p, axis=-1, keepdims=True), jnp.max(s_n, axis=-1, keepdims=True))
    p_p = jnp.exp(s_p - m)
    p_n = jnp.exp(s_n - m)
    l = jnp.sum(p_p, axis=-1, keepdims=True) + jnp.sum(p_n, axis=-1, keepdims=True)
    v_t = v_buf[slot].reshape(w, past).astype(BF16)
    o = (_dot_nt(p_p.astype(BF16), v_t) + _dot(p_n.astype(BF16), vn_ref[...].astype(BF16))) / l
    o_ref[...] = _unstack_heads(o, MOBA_HEADS, MOBA_DIM).astype(BF16)


def _moba_sample_call(page_table, mq_s, mk_s, mv_s, cache_k, cache_v, layer):
    db, n_pages = page_table.shape
    t = mq_s.shape[0] // db
    past = n_pages * PAGE_SIZE
    w = MOBA_HEADS * MOBA_DIM
    assert past % MOBA_BLOCK == 0 and past >= MOBA_BLOCK and t <= MOBA_BLOCK
    ck = jnp.transpose(cache_k, (0, 1, 3, 4, 2))
    cv = jnp.transpose(cache_v, (0, 1, 3, 4, 2))
    seq_block = pl.BlockSpec((None, t, w), lambda b, pt: (b, 0, 0))
    grid_spec = pltpu.PrefetchScalarGridSpec(
        num_scalar_prefetch=1,
        grid=(db,),
        in_specs=[seq_block, seq_block, seq_block,
                  pl.BlockSpec(memory_space=pl.ANY), pl.BlockSpec(memory_space=pl.ANY)],
        out_specs=seq_block,
        scratch_shapes=[pltpu.VMEM((2, MOBA_HEADS, MOBA_DIM, past), F32),
                        pltpu.VMEM((2, MOBA_HEADS, MOBA_DIM, past), F32),
                        pltpu.SemaphoreType.DMA((2,))])
    r3 = lambda a: a.reshape(db, t, w)
    return pl.pallas_call(
        functools.partial(_moba_sample_kernel, layer=layer, n_pages=n_pages),
        grid_spec=grid_spec,
        out_shape=jax.ShapeDtypeStruct((db, t, w), BF16),
        compiler_params=_cparams(("arbitrary",)),
        name="moba_sample",
    )(page_table, r3(mq_s), r3(mk_s), r3(mv_s), ck, cv).reshape(db * t, w)


def _merge_kernel(x_ref, a_ref, b_ref, c_ref, g_ref, wpa_ref, wpb_ref, wpc_ref, wout_ref, gffn_ref, wq_ref,
                  xo_ref, hnt_ref, qp_ref):
    d = x_ref.shape[1]
    m = (g_ref[:, 0:d].astype(F32) * _dot(a_ref[...], wpa_ref[...])
         + g_ref[:, d:2 * d].astype(F32) * _dot(b_ref[...], wpb_ref[...])
         + g_ref[:, 2 * d:3 * d].astype(F32) * _dot(c_ref[...], wpc_ref[...]))
    x = x_ref[...] + _dot(m.astype(BF16), wout_ref[...])
    xo_ref[...] = x
    hn32 = _rms(x, gffn_ref[...])
    hn = hn32.astype(BF16)
    hnt_ref[...] = hn32.T.astype(BF16)
    wq_h = qp_ref.shape[2]
    for h in range(qp_ref.shape[0]):
        qp_ref[h] = _dot(hn, wq_ref[:, h * wq_h:(h + 1) * wq_h])


def _merge_call(x_all, a_all, b_all, c_all, gates, wpa_pad, wpb, wpc, wout, g_ffn, wq):
    t, d = x_all.shape
    tm = TOK_TILE
    row = lambda w: pl.BlockSpec((tm, w), lambda i: (i, 0))
    full = lambda a: pl.BlockSpec(a.shape, lambda i: (0,) * a.ndim)
    wq_h = wq.shape[1] // PEER_HEADS
    return pl.pallas_call(
        _merge_kernel,
        grid=(t // tm,),
        in_specs=[row(d), row(a_all.shape[1]), row(b_all.shape[1]), row(c_all.shape[1]), row(3 * d),
                  full(wpa_pad), full(wpb), full(wpc), full(wout), full(g_ffn), full(wq)],
        out_specs=[row(d), pl.BlockSpec((d, tm), lambda i: (0, i)),
                   pl.BlockSpec((PEER_HEADS, tm, wq_h), lambda i: (0, i, 0))],
        out_shape=[jax.ShapeDtypeStruct((t, d), F32), jax.ShapeDtypeStruct((d, t), BF16),
                   jax.ShapeDtypeStruct((PEER_HEADS, t, wq_h), F32)],
        compiler_params=_cparams(("parallel",)),
        name="merge",
    )(x_all, a_all, b_all, c_all, gates, wpa_pad, wpb, wpc, wout, g_ffn, wq)


def _extract_rounds(s, order, exact, want_rank):
    orig = s
    vals = []
    rank = jnp.full(s.shape, float(PEER_TOPK), F32)
    for r in range(PEER_TOPK):
        m = jnp.max(s, axis=0, keepdims=True)
        hit = s == m
        if exact:
            idx = jnp.min(jnp.where(hit, order, 1e9), axis=0, keepdims=True)
            hit = order == idx
            if want_rank:
                rank = jnp.where(hit, float(r), rank)
        s = jnp.where(hit, -jnp.inf, s)
        vals.append(m)
    taken = (s == -jnp.inf) & (orig > -jnp.inf)
    if not want_rank:
        return jnp.concatenate(vals, axis=0), jnp.where(taken, 1.0, 0.0)
    if not exact:
        above = jnp.zeros(s.shape, F32)
        for r in range(PEER_TOPK):
            above = above + jnp.where(vals[r] > orig, 1.0, 0.0)
        rank = jnp.where(taken, above, float(PEER_TOPK))
    return jnp.concatenate(vals, axis=0), rank


def _extract_many(arrays, order, want_rank):
    fast = [_extract_rounds(s, order, False, want_rank) for s in arrays]
    clean = None
    for _, info in fast:
        n_taken = jnp.sum(jnp.where(info < float(PEER_TOPK), 1.0, 0.0) if want_rank else info, axis=0, keepdims=True)
        c = jnp.min(jnp.where(n_taken == float(PEER_TOPK), 1.0, 0.0))
        clean = c if clean is None else jnp.minimum(clean, c)
    flat = lax.cond(clean > 0.5,
                    lambda: tuple(x for pair in fast for x in pair),
                    lambda: tuple(x for s in arrays for x in _extract_rounds(s, order, True, want_rank)))
    return [(flat[2 * i], flat[2 * i + 1]) for i in range(len(arrays))]


def _select_heads(qs, keys):
    k = PEER_TOPK
    ka = 4
    tl = qs[0].shape[0]
    half = qs[0].shape[1] // 2
    scores = []
    for q, (k0, k1) in zip(qs, keys):
        scores += [_dot3_nt(k0, q[:, :half]), _dot3_nt(k1, q[:, half:])]
    rows = lax.broadcasted_iota(jnp.int32, scores[0].shape, 0).astype(F32)
    level1 = _extract_many(scores, rows, True)

    pos = lax.broadcasted_iota(jnp.int32, (ka * k, tl), 0)
    hi, lo = pos // k, pos % k
    ok_a = (hi + 1) * (lo + 1) <= k
    ok_b = ok_a & (lo >= ka)
    order = jnp.concatenate([jnp.where(ok_a, pos, 1000 + pos), jnp.where(ok_b, lo * k + hi, 2000 + pos)],
                            axis=0).astype(F32)
    cands = []
    for h in range(len(qs)):
        sv0, sv1 = level1[2 * h][0], level1[2 * h + 1][0]
        cand_a = (sv0[:ka, None, :] + sv1[None, :, :]).reshape(ka * k, tl)
        cand_b = (sv0[None, :, :] + sv1[:ka, None, :]).reshape(ka * k, tl)
        cands.append(jnp.concatenate([jnp.where(ok_a, cand_a, -jnp.inf), jnp.where(ok_b, cand_b, -jnp.inf)], axis=0))
    level2 = _extract_many(cands, order, False)

    out = []
    for h in range(len(qs)):
        (sv0, rank0), (sv1, rank1) = level1[2 * h], level1[2 * h + 1]
        top_s, picked = level2[h]
        z = jnp.sum(jnp.exp(top_s - top_s[0:1, :]), axis=0, keepdims=True)
        cnt_a = jnp.sum(picked[:ka * k].reshape(ka, k, tl), axis=1)
        cnt_b = jnp.sum(picked[ka * k:].reshape(ka, k, tl), axis=0)
        count = jnp.zeros(rank0.shape, F32)
        for r in range(k):
            cr = cnt_a[r:r + 1, :] if r < ka else cnt_b[r:r + 1, :]
            count = count + jnp.where(rank0 == float(r), cr, 0.0)
        s0, s1 = scores[2 * h], scores[2 * h + 1]
        out.append((count, jnp.exp(s0 - sv0[0:1, :]), rank1.astype(BF16),
                    (jnp.exp(s1 - sv1[0:1, :]) / z).astype(BF16)))
    return out


def _peer_gate_kernel(q_ref, keys_ref, g_ref, row_ref, map_ref):
    heads, tl = q_ref.shape[0], q_ref.shape[1]
    n_keys = keys_ref.shape[2]

    def per_head_group(hg, carry):
        hs = [hg * SEL_HEAD_GROUP + k for k in range(SEL_HEAD_GROUP)]
        maps = _select_heads([q_ref[h] for h in hs], [(keys_ref[h, 0], keys_ref[h, 1]) for h in hs])
        for h, (count, a, rank1, b) in zip(hs, maps):
            row_ref[h, 0] = count
            row_ref[h, 1] = a
            map_ref[h, 0] = rank1
            map_ref[h, 1] = b
        return carry

    lax.fori_loop(0, heads // SEL_HEAD_GROUP, per_head_group, 0)

    sub = _BF16_SUBLANES
    shape3 = (n_keys // sub, sub, tl)

    def per_row(i, carry):
        gate = jnp.zeros(shape3, BF16)
        for h in range(heads):
            cnt = jnp.broadcast_to(row_ref[h, 0, pl.ds(i, 1), :], (sub, tl)).astype(BF16)
            a = jnp.broadcast_to(row_ref[h, 1, pl.ds(i, 1), :], (sub, tl)).astype(BF16)
            b = map_ref[h, 1].reshape(shape3)
            gate = gate + jnp.where(map_ref[h, 0].reshape(shape3) < cnt[None], a[None] * b, jnp.zeros_like(b))
        g_ref[pl.ds(pl.multiple_of(i * n_keys, n_keys), n_keys), :] = gate.reshape(n_keys, tl)
        return carry

    lax.fori_loop(0, n_keys, per_row, 0)


def _peer_gate_call(qp, keys):
    heads, t, wq_h = qp.shape
    _, _, n_keys, half = keys.shape
    tl = SEL_TILE
    return pl.pallas_call(
        _peer_gate_kernel,
        grid=(t // tl,),
        in_specs=[pl.BlockSpec((heads, tl, wq_h), lambda i: (0, i, 0)),
                  pl.BlockSpec(keys.shape, lambda i: (0, 0, 0, 0))],
        out_specs=pl.BlockSpec((n_keys * n_keys, tl), lambda i: (0, i)),
        out_shape=jax.ShapeDtypeStruct((n_keys * n_keys, t), BF16),
        scratch_shapes=[pltpu.VMEM((heads, 2, n_keys, tl), F32), pltpu.VMEM((heads, 2, n_keys, tl), BF16)],
        compiler_params=_cparams(("parallel",)),
        name="peer_gate",
    )(qp, keys)


def _peer_expert_kernel(hnt_ref, x_ref, g_ref, u_ref, vt_ref, gfin_ref, o_ref, acc_ref, *, final_norm):
    e = pl.program_id(1)

    @pl.when(e == 0)
    def _():
        acc_ref[...] = jnp.zeros_like(acc_ref)

    act = _gelu_tanh(_dot(u_ref[...], hnt_ref[...])).astype(BF16)
    acc_ref[...] += _dot(vt_ref[...], g_ref[...] * act)

    @pl.when(e == pl.num_programs(1) - 1)
    def _():
        x = x_ref[...] + acc_ref[...].T
        if final_norm:
            x = _rms(x, gfin_ref[...])
        o_ref[...] = x


def _peer_expert_call(hnt, x_all, gates, u_tab, v_tab_t, g_final, final_norm):
    t, d = x_all.shape
    tt = PEER_TOK_TILE
    te = PEER_EXPERT_TILE
    return pl.pallas_call(
        functools.partial(_peer_expert_kernel, final_norm=final_norm),
        grid=(t // tt, u_tab.shape[0] // te),
        in_specs=[pl.BlockSpec((d, tt), lambda i, e: (0, i)),
                  pl.BlockSpec((tt, d), lambda i, e: (i, 0)),
                  pl.BlockSpec((te, tt), lambda i, e: (e, i)),
                  pl.BlockSpec((te, d), lambda i, e: (e, 0)),
                  pl.BlockSpec((d, te), lambda i, e: (0, e)),
                  pl.BlockSpec((1, d), lambda i, e: (0, 0))],
        out_specs=pl.BlockSpec((tt, d), lambda i, e: (i, 0)),
        out_shape=jax.ShapeDtypeStruct((t, d), F32),
        scratch_shapes=[pltpu.VMEM((d, tt), F32)],
        compiler_params=_cparams(("parallel", "arbitrary")),
        name="peer_expert",
    )(hnt, x_all, gates, u_tab, v_tab_t, g_final)


def kernel(x_prompt, x_sample, cache_mla_latent, cache_mla_krope, cache_moba_k, cache_moba_v, page_table,
           g_mix, w_in, g_q_lat, w_uq, g_kv_lat, w_ukv, g_gm_v, b_gm_v, w_s, b_s,
           w_pa, w_pb, w_pc, w_out, g_ffn, w_peer_q, peer_keys, peer_u, peer_v, g_final):
    batch, seq, d = x_prompt.shape
    db, t_new, _ = x_sample.shape
    depth = w_in.shape[0]
    n_pages = page_table.shape[1]
    past = n_pages * PAGE_SIZE
    n_p, n_s = batch * seq, db * t_new
    assert n_p % TOK_TILE == 0 and n_s % TOK_TILE == 0 and GM_CHUNK % t_new == 0
    assert (n_p + n_s) % PEER_TOK_TILE == 0 and seq % ATT_TILE == 0

    pos = jnp.concatenate([jnp.tile(jnp.arange(seq, dtype=jnp.int32), batch),
                           jnp.tile(past + jnp.arange(t_new, dtype=jnp.int32), db)])
    tabs = _rope_tables(pos)
    x_all = jnp.concatenate([x_prompt.reshape(n_p, d), x_sample.reshape(n_s, d)], axis=0)
    row2 = lambda a: a.reshape(1, -1)
    kvl = w_ukv.shape[1]
    cache_pe_t = jnp.transpose(cache_mla_krope, (0, 1, 3, 2))

    outs = {k: [] for k in ("lat", "pe", "mk", "mv", "gv")}
    for l in range(depth):
        w_arr = _arrange_w_in(w_in[l])
        wuq, wuqr = _arrange_w_uq(w_uq[l])
        reps = GM_CHUNK // t_new
        w_small = w_s[l][:, :t_new, :t_new]
        eye = jnp.eye(reps, dtype=w_s.dtype)
        w_samp = jnp.einsum("ab,gts->gatbs", eye, w_small).reshape(GM_GROUPS, GM_CHUNK, GM_CHUNK)
        wmix = jnp.stack([w_s[l], w_samp])
        b_full = jnp.repeat(b_s[l].T, GM_DIM, axis=1)
        b_samp = jnp.tile(jnp.repeat(b_s[l][:, :t_new].T, GM_DIM, axis=1), (reps, 1))
        bmix = jnp.stack([b_full, b_samp])
        w3 = w_ukv[l].reshape(w_ukv.shape[1], MLA_HEADS, MLA_NOPE + MLA_V)
        wuk_pad = jnp.concatenate(
            [jnp.transpose(w3[..., :MLA_NOPE], (1, 2, 0)),
             jnp.zeros((MLA_HEADS, LANES - MLA_NOPE, w3.shape[0]), w3.dtype)], axis=1).astype(BF16)
        wuv_pad = jnp.concatenate(
            [jnp.zeros((MLA_HEADS, w3.shape[0], LANES - MLA_V), w3.dtype),
             jnp.transpose(w3[..., MLA_NOPE:], (1, 0, 2))], axis=2).astype(BF16)
        wpa3 = w_pa[l].reshape(MLA_HEADS, MLA_V, d)
        wpa_pad = jnp.concatenate([jnp.zeros((MLA_HEADS, LANES - MLA_V, d), w_pa.dtype), wpa3],
                                  axis=1).reshape(MLA_HEADS * LANES, d).astype(BF16)

        (ckv, kpe, mk, mv, gv, qall, kpep, mq, c_all, gates, ckvt, mvt) = _proj_call(
            x_all, tabs, n_p // TOK_TILE, row2(g_mix[l]), w_arr, row2(g_q_lat[l]), wuq, wuqr,
            row2(g_kv_lat[l]), row2(g_gm_v[l]), row2(b_gm_v[l]), wmix, bmix)

        a_p = _mla_prompt_call(qall, ckv, kpep, ckvt, wuk_pad, wuv_pad, batch, seq)
        b_p = _moba_prompt_call(mq, mk, mvt, batch, seq)
        a_s = _mla_sample_call(page_table, qall[n_p:], ckv[n_p:], kpep[n_p:], wuk_pad, wuv_pad,
                               cache_mla_latent, cache_pe_t, l)
        b_s_out = _moba_sample_call(page_table, mq[n_p:], mk[n_p:], mv[n_p:], cache_moba_k, cache_moba_v, l)
        a_all = jnp.concatenate([a_p, a_s], axis=0)
        b_all = jnp.concatenate([b_p, b_s_out], axis=0)

        x_mid, hnt, qp = _merge_call(x_all, a_all, b_all, c_all, gates, wpa_pad, w_pb[l].astype(BF16),
                                     w_pc[l].astype(BF16), w_out[l].astype(BF16), row2(g_ffn[l]),
                                     w_peer_q[l].astype(BF16))
        peer_gates = _peer_gate_call(qp, peer_keys[l])
        x_all = _peer_expert_call(hnt, x_mid, peer_gates, peer_u[l].astype(BF16),
                                  peer_v[l].T.astype(BF16), row2(g_final), l == depth - 1)

        outs["lat"].append(ckv)
        outs["pe"].append(kpe)
        outs["mk"].append(mk)
        outs["mv"].append(mv)
        outs["gv"].append(gv)

    def split(name, tail):
        st = jnp.stack(outs[name])
        return (st[:, :n_p].reshape((depth, batch, seq) + tail), st[:, n_p:].reshape((depth, db, t_new) + tail))

    lat_p, lat_s = split("lat", (kvl,))
    pe_p, pe_s = split("pe", (MLA_ROPE,))
    mk_p, mk_s = split("mk", (MOBA_HEADS, MOBA_DIM))
    mv_p, mv_s = split("mv", (MOBA_HEADS, MOBA_DIM))
    _, gv_s = split("gv", (GM_GROUPS, GM_DIM))
    y_prompt = x_all[:n_p].reshape(batch, seq, d)
    y_sample = x_all[n_p:].reshape(db, t_new, d)
    return (y_prompt, y_sample, lat_p, pe_p, mk_p, mv_p, lat_s, pe_s, mk_s, mv_s, gv_s)
```
